```python
import jax
import jax.numpy as jnp
from jax import lax
import numpy as np

D_MODEL = 1024
BATCH = 16
SEQ = 2048
DEPTH = 4

GRID_W = 64
CTX_LEN = 256
N_MIXERS = 4
MIX_CONV = 0
MIX_POOL = 1
MIX_MLA = 2
MIX_CHUNK = 3
BRANCH = D_MODEL
EPS = 1e-6
CONV_WIDTH = 31
POOL_WINDOWS = (2, 4, 8, 16)
POOL_GROUP = BRANCH // len(POOL_WINDOWS)
MLA_HEADS = D_MODEL // 128
MLA_NOPE = 128
MLA_ROPE = 64
MLA_V = 128
MLA_Q_RANK = 3 * D_MODEL // 8
MLA_KV_RANK = D_MODEL // 4
MLA_KVC = MLA_KV_RANK + MLA_ROPE
MLA_SCALE = (MLA_NOPE + MLA_ROPE) ** -0.5
ROPE_THETA = 10000.0
Q_BLOCK = 128
CHUNK = 128
CHUNK_GROUPS = 8
CHUNK_GC = BRANCH // CHUNK_GROUPS

kernel_name = 'hybrid_interleaved_diffusion_block'


def _n_layers_of(kind):
    return len(range(kind, DEPTH, N_MIXERS))


def _rms(x, g):
    xf = x.astype(jnp.float32)
    y = xf * lax.rsqrt(jnp.mean(xf * xf, axis=-1, keepdims=True) + EPS)
    return (y * g.astype(jnp.float32)).astype(x.dtype)


def _layernorm(x, g, b):
    xf = x.astype(jnp.float32)
    mu = jnp.mean(xf, axis=-1, keepdims=True)
    var = jnp.mean(jnp.square(xf - mu), axis=-1, keepdims=True)
    y = (xf - mu) * lax.rsqrt(var + EPS)
    return (y * g.astype(jnp.float32) + b.astype(jnp.float32)).astype(x.dtype)


def _rope_tables(rows):
    row_id = jnp.repeat(jnp.arange(rows), GRID_W).astype(jnp.float32)
    col_id = jnp.tile(jnp.arange(GRID_W), rows).astype(jnp.float32)
    axis_dim = MLA_ROPE // 2
    freqs = ROPE_THETA ** (-jnp.arange(0, axis_dim, 2, dtype=jnp.float32) / axis_dim)
    ar = row_id[:, None] * freqs
    ac = col_id[:, None] * freqs
    return (jnp.cos(ar), jnp.sin(ar), jnp.cos(ac), jnp.sin(ac))


def _rope1d(x, cos, sin):
    x1, x2 = jnp.split(x, 2, axis=-1)
    return jnp.concatenate([x1 * cos - x2 * sin, x1 * sin + x2 * cos], axis=-1)


def _rope2d(x, tabs):
    cr, sr, cc, sc = [t.astype(x.dtype) for t in tabs]
    if x.ndim == 4:
        cr, sr, cc, sc = cr[:, None], sr[:, None], cc[:, None], sc[:, None]
    xr, xc = jnp.split(x, 2, axis=-1)
    return jnp.concatenate([_rope1d(xr, cr, sr), _rope1d(xc, cc, sc)], axis=-1)


def _conv_mixer(h, w_in, dw, db, ln_g, ln_b, w_out):
    a, b, g = jnp.split(h @ w_in, 3, axis=-1)
    y = a * jax.nn.sigmoid(b)
    y = lax.conv_general_dilated(
        y, dw[:, None, :].astype(y.dtype), window_strides=(1,),
        padding=[(CONV_WIDTH // 2, CONV_WIDTH // 2)],
        dimension_numbers=('NWC', 'WIO', 'NWC'),
        feature_group_count=BRANCH) + db
    y = jax.nn.silu(_layernorm(y, ln_g, ln_b)) * jax.nn.silu(g)
    return y @ w_out


def _window_mean(v, w):
    L = v.shape[1]
    cs = jnp.pad(jnp.cumsum(v.astype(jnp.float32), axis=1), ((0, 0), (1, 0), (0, 0)))
    t = jnp.arange(L)
    start = jnp.clip(t - w // 2, 0, L)
    end = jnp.clip(t + (w - w // 2), 0, L)
    total = jnp.take(cs, end, axis=1) - jnp.take(cs, start, axis=1)
    cnt = (end - start).astype(jnp.float32)
    return (total / cnt[None, :, None]).astype(v.dtype)


def _pool_mixer(h, w_in, w_grp, scale, w_out):
    v, g = jnp.split(h @ w_in, 2, axis=-1)
    B, L, _ = v.shape
    vg = v.reshape(B, L, len(POOL_WINDOWS), POOL_GROUP)
    pooled = jnp.stack([_window_mean(vg[:, :, k], w) for k, w in enumerate(POOL_WINDOWS)], axis=2) - vg
    y = jnp.einsum('blgc,gcd->blgd', pooled, w_grp).reshape(B, L, BRANCH) * scale
    return (y * jax.nn.silu(g)) @ w_out


def _mla_keys(pkv, kv_norm, w_ukv, k_nope_g, k_rope_g, tabs):
    ckv, kr = jnp.split(pkv, [MLA_KV_RANK], axis=-1)
    B, L, _ = ckv.shape
    kv = (_rms(ckv, kv_norm) @ w_ukv).reshape(B, L, MLA_HEADS, MLA_NOPE + MLA_V)
    kn, v = jnp.split(kv, [MLA_NOPE], axis=-1)
    kn = _rms(kn, k_nope_g)
    kr = _rms(kr, k_rope_g)
    if tabs is not None:
        kr = _rope2d(kr, tabs)
    return kn, kr, v


def _mla_queries(cq, q_norm, w_uq, q_nope_g, q_rope_g, tabs):
    B, L, _ = cq.shape
    q = (_rms(cq, q_norm) @ w_uq).reshape(B, L, MLA_HEADS, MLA_NOPE + MLA_ROPE)
    qn, qr = jnp.split(q, [MLA_NOPE], axis=-1)
    qn = _rms(qn, q_nope_g)
    qr = _rms(qr, q_rope_g)
    if tabs is not None:
        qr = _rope2d(qr, tabs)
    return qn, qr


def _attend(qn, qr, kn, kr, v):
    s = (jnp.einsum('bqhd,bkhd->bhqk', qn, kn)
         + jnp.einsum('bqhr,bkr->bhqk', qr, kr)).astype(jnp.float32) * MLA_SCALE
    p = jax.nn.softmax(s, axis=-1).astype(v.dtype)
    return jnp.einsum('bhqk,bkhd->bqhd', p, v)


def _attend_blocks(qn, qr, kn, kr, v):
    B, L, H, _ = qn.shape
    nb = L // Q_BLOCK

    def blk(t):
        return jnp.moveaxis(t.reshape((B, nb, Q_BLOCK) + t.shape[2:]), 1, 0)

    out = lax.map(lambda q: _attend(q[0], q[1], kn, kr, v), (blk(qn), blk(qr)))
    return jnp.moveaxis(out, 0, 1).reshape(B, L, H * MLA_V)


def _mla_mixer(h, hc, with_ctx_out, tabs, w_in, q_norm, kv_norm, w_uq, w_ukv, nope_g, rope_g, w_out):
    pkv, cq, g = jnp.split(h @ w_in, [MLA_KVC, MLA_KVC + MLA_Q_RANK], axis=-1)
    kn, kr, v = _mla_keys(pkv, kv_norm, w_ukv, nope_g[1], rope_g[1], tabs)
    qn, qr = _mla_queries(cq, q_norm, w_uq, nope_g[0], rope_g[0], tabs)
    pc = hc @ (w_in if with_ctx_out else w_in[:, :MLA_KVC])
    knc, krc, vc = _mla_keys(pc[..., :MLA_KVC], kv_norm, w_ukv, nope_g[1], rope_g[1], None)
    o = _attend_blocks(qn, qr,
                       jnp.concatenate([kn, knc], axis=1),
                       jnp.concatenate([kr, krc], axis=1),
                       jnp.concatenate([v, vc], axis=1))
    out = (o * jax.nn.silu(g)) @ w_out
    out_c = None
    if with_ctx_out:
        cqc, gc = jnp.split(pc[..., MLA_KVC:], [MLA_Q_RANK], axis=-1)
        qnc, qrc = _mla_queries(cqc, q_norm, w_uq, nope_g[0], rope_g[0], None)
        Bc, Lc = hc.shape[0], hc.shape[1]
        oc = _attend(qnc, qrc, knc, krc, vc).reshape(Bc, Lc, MLA_HEADS * MLA_V)
        out_c = (oc * jax.nn.silu(gc)) @ w_out
    return out, out_c


def _chunk_mixer(h, w_in, ln_g, ln_b, w_s, b_s, w_out):
    u, v, g = jnp.split(h @ w_in, 3, axis=-1)
    B, L, _ = v.shape
    v = _layernorm(v, ln_g, ln_b).reshape(B, L // CHUNK, CHUNK, CHUNK_GROUPS, CHUNK_GC)
    s = jnp.einsum('gpq,bnqgc->bnpgc', w_s, v) + b_s[:, :, None]
    y = u * s.reshape(B, L, BRANCH) * jax.nn.silu(g)
    return y @ w_out


def setup_inputs(seed: int = 0) -> dict:
    key = jax.random.key(seed)
    ks = iter(jax.random.split(key, 40))

    def nrm(shape, s):
        return jax.random.normal(next(ks), shape, jnp.float32) * s

    nA, nB, nC, nD = (_n_layers_of(k) for k in range(N_MIXERS))
    D, E = D_MODEL, BRANCH
    HQK = MLA_HEADS * (MLA_NOPE + MLA_ROPE)
    HKV = MLA_HEADS * (MLA_NOPE + MLA_V)
    HV = MLA_HEADS * MLA_V
    return {
        'x': nrm((BATCH, SEQ, D), 1.0),
        'c': nrm((BATCH, D), 1.0),
        'ctx': nrm((BATCH, CTX_LEN, D), 1.0),
        'c_ctx': nrm((D,), 1.0),
        'norm_g': 1.0 + nrm((DEPTH, D), 0.05),
        'w_mod': nrm((DEPTH, D, 3 * D), 0.5 * D ** -0.5),
        'b_mod': nrm((DEPTH, 3 * D), 0.01),
        'cv_w_in': nrm((nA, D, 3 * E), D ** -0.5),
        'cv_dw': nrm((nA, CONV_WIDTH, E), CONV_WIDTH ** -0.5),
        'cv_db': nrm((nA, E), 0.01),
        'cv_ln_g': 1.0 + nrm((nA, E), 0.05),
        'cv_ln_b': nrm((nA, E), 0.01),
        'cv_w_out': nrm((nA, E, D), E ** -0.5),
        'pl_w_in': nrm((nB, D, 2 * E), D ** -0.5),
        'pl_w_grp': nrm((nB, len(POOL_WINDOWS), POOL_GROUP, POOL_GROUP), POOL_GROUP ** -0.5),
        'pl_scale': 1.0 + nrm((nB, E), 0.05),
        'pl_w_out': nrm((nB, E, D), E ** -0.5),
        'ml_w_in': nrm((nC, D, MLA_KVC + MLA_Q_RANK + HV), D ** -0.5),
        'ml_q_norm': 1.0 + nrm((nC, MLA_Q_RANK), 0.05),
        'ml_kv_norm': 1.0 + nrm((nC, MLA_KV_RANK), 0.05),
        'ml_w_uq': nrm((nC, MLA_Q_RANK, HQK), MLA_Q_RANK ** -0.5),
        'ml_w_ukv': nrm((nC, MLA_KV_RANK, HKV), MLA_KV_RANK ** -0.5),
        'ml_nope_norm': 1.0 + nrm((nC, 2, MLA_NOPE), 0.05),
        'ml_rope_norm': 1.0 + nrm((nC, 2, MLA_ROPE), 0.05),
        'ml_w_out': nrm((nC, HV, D), HV ** -0.5),
        'ch_w_in': nrm((nD, D, 3 * E), D ** -0.5),
        'ch_ln_g': 1.0 + nrm((nD, E), 0.05),
        'ch_ln_b': nrm((nD, E), 0.01),
        'ch_w_s': nrm((nD, CHUNK_GROUPS, CHUNK, CHUNK), CHUNK ** -0.5),
        'ch_b_s': 1.0 + nrm((nD, CHUNK, CHUNK_GROUPS), 0.05),
        'ch_w_out': nrm((nD, E, D), E ** -0.5),
    }


def reference(x, c, ctx, c_ctx, norm_g, w_mod, b_mod,
              cv_w_in, cv_dw, cv_db, cv_ln_g, cv_ln_b, cv_w_out,
              pl_w_in, pl_w_grp, pl_scale, pl_w_out,
              ml_w_in, ml_q_norm, ml_kv_norm, ml_w_uq, ml_w_ukv, ml_nope_norm, ml_rope_norm, ml_w_out,
              ch_w_in, ch_ln_g, ch_ln_b, ch_w_s, ch_b_s, ch_w_out):
    L = x.shape[1]
    ROWS = L // GRID_W
    tabs = _rope_tables(ROWS)
    s_lat = jax.nn.silu(c)
    s_ctx = jax.nn.silu(c_ctx)
    cx = ctx
    for i in range(DEPTH):
        kind, j = i % N_MIXERS, i // N_MIXERS
        ctx_out = any(k % N_MIXERS == MIX_MLA for k in range(i + 1, DEPTH))
        ctx_in = ctx_out or kind == MIX_MLA
        sh, sc, gt = jnp.split((s_lat @ w_mod[i] + b_mod[i])[:, None, :], 3, axis=-1)
        h = _rms(x, norm_g[i]) * (1.0 + sc) + sh
        if ctx_in:
            shc, scc, gtc = jnp.split(s_ctx @ w_mod[i] + b_mod[i], 3, axis=-1)
            hc = _rms(cx, norm_g[i]) * (1.0 + scc) + shc
        if kind == MIX_CONV:
            args = (cv_w_in[j], cv_dw[j], cv_db[j], cv_ln_g[j], cv_ln_b[j], cv_w_out[j])
            o = _conv_mixer(h, *args)
            oc = _conv_mixer(hc, *args) if ctx_out else None
        elif kind == MIX_POOL:
            args = (pl_w_in[j], pl_w_grp[j], pl_scale[j], pl_w_out[j])
            o = _pool_mixer(h, *args)
            oc = _pool_mixer(hc, *args) if ctx_out else None
        elif kind == MIX_MLA:
            o, oc = _mla_mixer(h, hc, ctx_out, tabs, ml_w_in[j], ml_q_norm[j], ml_kv_norm[j],
                               ml_w_uq[j], ml_w_ukv[j], ml_nope_norm[j], ml_rope_norm[j], ml_w_out[j])
        else:
            args = (ch_w_in[j], ch_ln_g[j], ch_ln_b[j], ch_w_s[j], ch_b_s[j], ch_w_out[j])
            o = _chunk_mixer(h, *args)
            oc = _chunk_mixer(hc, *args) if ctx_out else None
        x = x + gt * o
        if ctx_out:
            cx = cx + gtc * oc
    return x
```

```python
import functools

import numpy as np
import jax
import jax.numpy as jnp
from jax import lax
from jax.experimental import pallas as pl
from jax.experimental.pallas import tpu as pltpu

D = 1024
DEPTH = 4
EPS = 1e-6
GRID_W = 64
CONV_WIDTH = 31
CONV_HALO = 16
POOL_WINDOWS = (2, 4, 8, 16)
POOL_GROUP = D // len(POOL_WINDOWS)
POOL_HALO = 8
HEADS = 8
NOPE = 128
ROPE = 64
HEAD_V = 128
HEAD_QK = 256
Q_RANK = 384
KV_RANK = 256
ROPE_THETA = 10000.0
SCORE_SCALE = (NOPE + ROPE) ** -0.5
LOG2E = 1.4426950408889634
CHUNK = 128
CHUNK_GROUPS = 8
SEQ_TILE = 512
ATTN_Q_TILE = 512
VMEM_LIMIT = 56 * 1024 * 1024

F32 = jnp.float32
BF16 = jnp.bfloat16


def _sigmoid(x):
    return 0.5 * jnp.tanh(0.5 * x) + 0.5


def _silu(x):
    return x * _sigmoid(x)


def _dot(a, b):
    return jnp.dot(a, b, preferred_element_type=F32)


def _dot_nt(a, b):
    return lax.dot_general(a, b, (((1,), (1,)), ((), ())), preferred_element_type=F32)


def _rms(x, g, n=None):
    n = x.shape[-1] if n is None else n
    ms = jnp.sum(x * x, axis=-1, keepdims=True) * (1.0 / n)
    return (x * lax.rsqrt(ms + EPS)) * g


def _layernorm(x, g, b):
    mu = jnp.mean(x, axis=-1, keepdims=True)
    xc = x - mu
    var = jnp.mean(xc * xc, axis=-1, keepdims=True)
    return (xc * lax.rsqrt(var + EPS)) * g + b


def _modulated(x, ng, mod):
    return _rms(x, ng) * (1.0 + mod[1:2]) + mod[0:1]


def _const_spec(shape):
    nd = len(shape)
    return pl.BlockSpec(shape, lambda *_: (0,) * nd)


def _params(n_axes):
    return pltpu.CompilerParams(dimension_semantics=("arbitrary",) * n_axes,
                                vmem_limit_bytes=VMEM_LIMIT)


def _mod_kernel(cv_ref, w_ref, b_ref, o_ref):
    s = _silu(cv_ref[...])
    o_ref[0] = _dot(s.astype(BF16), w_ref[0].astype(BF16)) + b_ref[0]


def _modulation(cvec, w_mod, b_mod):
    rows = cvec.shape[0]
    nt = 3
    return pl.pallas_call(
        _mod_kernel,
        out_shape=jax.ShapeDtypeStruct((DEPTH, rows, 3 * D), F32),
        grid=(DEPTH, nt),
        in_specs=[_const_spec((rows, D)),
                  pl.BlockSpec((1, D, D), lambda i, n: (i, 0, n)),
                  pl.BlockSpec((1, 1, D), lambda i, n: (i, 0, n))],
        out_specs=pl.BlockSpec((1, rows, D), lambda i, n: (i, 0, n)),
        compiler_params=_params(2),
        name="modulation",
    )(cvec, w_mod, b_mod.reshape(DEPTH, 1, 3 * D))


def _halo_specs(tl, seq, halo):
    per = tl // halo
    last = seq // halo - 1
    prev = pl.BlockSpec((1, halo, D), lambda b, j: (b, jnp.maximum(j * per - 1, 0), 0))
    main = pl.BlockSpec((1, tl, D), lambda b, j: (b, j, 0))
    nxt = pl.BlockSpec((1, halo, D), lambda b, j: (b, jnp.minimum((j + 1) * per, last), 0))
    return prev, main, nxt


def _row_valid(tl, seq, halo):
    e = lax.broadcasted_iota(jnp.int32, (tl + 2 * halo, 1), 0)
    t = pl.program_id(1) * tl - halo + e
    return (t >= 0) & (t < seq)


def _conv_kernel(xp_ref, xm_ref, xn_ref, mod_ref, ng_ref, win_ref, dw_ref, db_ref,
                 lng_ref, lnb_ref, wout_ref, o_ref, ybuf, zbuf, *, tl, seq):
    halo = CONV_HALO
    mod = mod_ref[0]
    xm = xm_ref[0]
    x_ext = jnp.concatenate([xp_ref[0], xm, xn_ref[0]], axis=0)
    h = _modulated(x_ext, ng_ref[...], mod).astype(BF16)
    a = _dot(h, win_ref[:, 0:D])
    b = _dot(h, win_ref[:, D:2 * D])
    y = a * _sigmoid(b)
    ybuf[...] = jnp.where(_row_valid(tl, seq, halo), y, 0.0)
    sgate = _silu(_dot(h[halo:halo + tl], win_ref[:, 2 * D:3 * D]))

    rb = min(tl, 256)
    lb = 128
    base = halo - CONV_WIDTH // 2
    for r0 in range(0, tl, rb):
        for c0 in range(0, D, lb):
            acc = jnp.zeros((rb, lb), F32)
            for k in range(CONV_WIDTH):
                acc = acc + dw_ref[k:k + 1, c0:c0 + lb] * ybuf[r0 + base + k:r0 + base + k + rb, c0:c0 + lb]
            zbuf[r0:r0 + rb, c0:c0 + lb] = acc
    z = zbuf[...] + db_ref[...]
    u = _silu(_layernorm(z, lng_ref[...], lnb_ref[...])) * sgate
    o_ref[0] = xm + mod[2:3] * _dot(u.astype(BF16), wout_ref[...])


def _conv_layer(x, mod, ng, w_in, dw, db, ln_g, ln_b, w_out):
    bsz, seq, _ = x.shape
    tl = min(SEQ_TILE, seq)
    prev, main, nxt = _halo_specs(tl, seq, CONV_HALO)
    return pl.pallas_call(
        functools.partial(_conv_kernel, tl=tl, seq=seq),
        out_shape=jax.ShapeDtypeStruct(x.shape, F32),
        grid=(bsz, seq // tl),
        in_specs=[prev, main, nxt,
                  pl.BlockSpec((1, 3, D), lambda b, j: (b, 0, 0)),
                  _const_spec((1, D)),
                  _const_spec((D, 3 * D)),
                  _const_spec((CONV_WIDTH, D)),
                  _const_spec((1, D)), _const_spec((1, D)), _const_spec((1, D)),
                  _const_spec((D, D))],
        out_specs=main,
        scratch_shapes=[pltpu.VMEM((tl + 2 * CONV_HALO, D), F32),
                        pltpu.VMEM((tl, D), F32)],
        compiler_params=_params(2),
        name="conv_layer",
    )(x, x, x, mod, ng, w_in, dw, db, ln_g, ln_b, w_out)


def _pool_kernel(xp_ref, xm_ref, xn_ref, mod_ref, ng_ref, win_ref, wgrp_ref, scale_ref,
                 wout_ref, o_ref, vbuf, *, tl, seq):
    halo = POOL_HALO
    mod = mod_ref[0]
    xm = xm_ref[0]
    x_ext = jnp.concatenate([xp_ref[0], xm, xn_ref[0]], axis=0)
    h = _modulated(x_ext, ng_ref[...], mod).astype(BF16)
    v = _dot(h, win_ref[:, 0:D])
    vbuf[...] = jnp.where(_row_valid(tl, seq, halo), v, 0.0)
    sgate = _silu(_dot(h[halo:halo + tl], win_ref[:, D:2 * D]))

    t = pl.program_id(1) * tl + lax.broadcasted_iota(jnp.int32, (tl, 1), 0)
    ys = []
    for k, w in enumerate(POOL_WINDOWS):
        c0 = k * POOL_GROUP
        tot = jnp.zeros((tl, POOL_GROUP), F32)
        for off in range(-(w // 2), w - w // 2):
            tot = tot + vbuf[halo + off:halo + off + tl, c0:c0 + POOL_GROUP]
        cnt = (jnp.minimum(t + (w - w // 2), seq) - jnp.maximum(t - w // 2, 0)).astype(F32)
        pooled = tot / cnt - vbuf[halo:halo + tl, c0:c0 + POOL_GROUP]
        ys.append(_dot(pooled.astype(BF16), wgrp_ref[k]))
    y = jnp.concatenate(ys, axis=1) * scale_ref[...]
    o_ref[0] = xm + mod[2:3] * _dot((y * sgate).astype(BF16), wout_ref[...])


def _pool_layer(x, mod, ng, w_in, w_grp, scale, w_out):
    bsz, seq, _ = x.shape
    tl = min(SEQ_TILE, seq)
    prev, main, nxt = _halo_specs(tl, seq, POOL_HALO)
    return pl.pallas_call(
        functools.partial(_pool_kernel, tl=tl, seq=seq),
        out_shape=jax.ShapeDtypeStruct(x.shape, F32),
        grid=(bsz, seq // tl),
        in_specs=[prev, main, nxt,
                  pl.BlockSpec((1, 3, D), lambda b, j: (b, 0, 0)),
                  _const_spec((1, D)),
                  _const_spec((D, 2 * D)),
                  _const_spec((len(POOL_WINDOWS), POOL_GROUP, POOL_GROUP)),
                  _const_spec((1, D)),
                  _const_spec((D, D))],
        out_specs=main,
        scratch_shapes=[pltpu.VMEM((tl + 2 * POOL_HALO, D), F32)],
        compiler_params=_params(2),
        name="pool_layer",
    )(x, x, x, mod, ng, w_in, w_grp, scale, w_out)


def _chunk_kernel(x_ref, mod_ref, ng_ref, win_ref, lng_ref, lnb_ref, ws_ref, bs_ref,
                  wout_ref, o_ref, *, tl):
    mod = mod_ref[0]
    x = x_ref[0]
    h = _modulated(x, ng_ref[...], mod).astype(BF16)
    u = _dot(h, win_ref[:, 0:D])
    v = _layernorm(_dot(h, win_ref[:, D:2 * D]), lng_ref[...], lnb_ref[...]).astype(BF16)
    sgate = _silu(_dot(h, win_ref[:, 2 * D:3 * D]))
    nch = tl // CHUNK
    gc = D // CHUNK_GROUPS
    cols = []
    for g in range(CHUNK_GROUPS):
        vg = jnp.concatenate([v[n * CHUNK:(n + 1) * CHUNK, g * gc:(g + 1) * gc] for n in range(nch)], axis=1)
        sg = _dot(ws_ref[g], vg)
        cols.append(jnp.concatenate([sg[:, n * gc:(n + 1) * gc] for n in range(nch)], axis=0))
    s = jnp.concatenate(cols, axis=1) + jnp.concatenate([bs_ref[...]] * nch, axis=0)
    y = u * s * sgate
    o_ref[0] = x + mod[2:3] * _dot(y.astype(BF16), wout_ref[...])


def _chunk_layer(x, mod, ng, w_in, ln_g, ln_b, w_s, b_s, w_out):
    bsz, seq, _ = x.shape
    tl = min(SEQ_TILE, seq)
    main = pl.BlockSpec((1, tl, D), lambda b, j: (b, j, 0))
    return pl.pallas_call(
        functools.partial(_chunk_kernel, tl=tl),
        out_shape=jax.ShapeDtypeStruct(x.shape, F32),
        grid=(bsz, seq // tl),
        in_specs=[main,
                  pl.BlockSpec((1, 3, D), lambda b, j: (b, 0, 0)),
                  _const_spec((1, D)),
                  _const_spec((D, 3 * D)),
                  _const_spec((1, D)), _const_spec((1, D)),
                  _const_spec((CHUNK_GROUPS, CHUNK, CHUNK)),
                  _const_spec((CHUNK, D)),
                  _const_spec((D, D))],
        out_specs=main,
        compiler_params=_params(2),
        name="chunk_layer",
    )(x, mod, ng, w_in, ln_g, ln_b, w_s, b_s, w_out)


def _rope_tables(seq):
    rows = seq // GRID_W
    row_id = np.repeat(np.arange(rows), GRID_W).astype(np.float32)
    col_id = np.tile(np.arange(GRID_W), rows).astype(np.float32)
    axis_dim = ROPE // 2
    freqs = (np.float32(ROPE_THETA) ** (-np.arange(0, axis_dim, 2, dtype=np.float32) / axis_dim)).astype(np.float32)
    ar = row_id[:, None] * freqs
    ac = col_id[:, None] * freqs
    q = axis_dim // 2
    zeros = np.zeros((seq, q), np.float32)
    pad = np.zeros((seq, 128 - ROPE), np.float32)
    cos = np.concatenate([np.cos(ar), np.cos(ar), np.cos(ac), np.cos(ac), pad], axis=1)
    sin_up = np.concatenate([-np.sin(ar), zeros, -np.sin(ac), zeros, pad], axis=1)
    sin_dn = np.concatenate([zeros, np.sin(ar), zeros, np.sin(ac), pad], axis=1)
    return (jnp.asarray(cos, F32), jnp.asarray(sin_up, F32), jnp.asarray(sin_dn, F32))


def _rope(x, cos, sin_up, sin_dn):
    quarter = ROPE // 4
    up = pltpu.roll(x, 128 - quarter, axis=1)
    dn = pltpu.roll(x, quarter, axis=1)
    return x * cos + up * sin_up + dn * sin_dn


def _mla_keys(ckv, kr, kvn_ref, wukv_ref, kng_ref, krg_ref, tabs, k_ref, v_ref):
    kv = _dot(_rms(ckv, kvn_ref[...]).astype(BF16), wukv_ref[...])
    krn = _rms(kr, krg_ref[...], n=ROPE)
    if tabs is not None:
        krn = _rope(krn, *tabs)
    krn = krn.astype(BF16)
    kng = kng_ref[...]
    for hd in range(HEADS):
        c0 = hd * HEAD_QK
        k_ref[0, :, c0:c0 + NOPE] = _rms(kv[:, c0:c0 + NOPE], kng).astype(BF16)
        k_ref[0, :, c0 + NOPE:c0 + HEAD_QK] = krn
        v_ref[0, :, hd * HEAD_V:(hd + 1) * HEAD_V] = kv[:, c0 + NOPE:c0 + HEAD_QK].astype(BF16)


def _mla_proj_kernel(x_ref, mod_ref, ng_ref, win_ref, qn_ref, kvn_ref, wuq_ref, wukv_ref,
                     qng_ref, qrg_ref, kng_ref, krg_ref, cos_ref, sup_ref, sdn_ref,
                     q_ref, k_ref, v_ref, sg_ref):
    h = _modulated(x_ref[0], ng_ref[...], mod_ref[0]).astype(BF16)
    p = _dot(h, win_ref[...])
    tabs = (cos_ref[...], sup_ref[...], sdn_ref[...])
    _mla_keys(p[:, 0:KV_RANK], p[:, KV_RANK:KV_RANK + 128], kvn_ref, wukv_ref, kng_ref, krg_ref,
              tabs, k_ref, v_ref)
    c1 = KV_RANK + 128
    q = _dot(_rms(p[:, c1:c1 + Q_RANK], qn_ref[...]).astype(BF16), wuq_ref[...])
    qng = qng_ref[...]
    qrg = qrg_ref[...]
    qs = SCORE_SCALE * LOG2E
    for hd in range(HEADS):
        c0 = hd * HEAD_QK
        q_ref[0, :, c0:c0 + NOPE] = (_rms(q[:, c0:c0 + NOPE], qng) * qs).astype(BF16)
        qr = _rope(_rms(q[:, c0 + NOPE:c0 + HEAD_QK], qrg, n=ROPE), *tabs)
        q_ref[0, :, c0 + NOPE:c0 + HEAD_QK] = (qr * qs).astype(BF16)
    sg_ref[0] = _silu(p[:, c1 + Q_RANK:c1 + Q_RANK + D]).astype(BF16)


def _mla_ctx_kernel(x_ref, mod_ref, ng_ref, win_ref, kvn_ref, wukv_ref, kng_ref, krg_ref,
                    k_ref, v_ref):
    h = _modulated(x_ref[0], ng_ref[...], mod_ref[0]).astype(BF16)
    p = _dot(h, win_ref[...])
    _mla_keys(p[:, 0:KV_RANK], p[:, KV_RANK:KV_RANK + 128], kvn_ref, wukv_ref, kng_ref, krg_ref,
              None, k_ref, v_ref)


def _attn_kernel(q_ref, kl_ref, vl_ref, kc_ref, vc_ref, o_ref):
    q = q_ref[0]
    s1 = _dot_nt(q, kl_ref[0])
    s2 = _dot_nt(q, kc_ref[0])
    m = jnp.maximum(jnp.max(s1, axis=-1, keepdims=True), jnp.max(s2, axis=-1, keepdims=True))
    p1 = jnp.exp2(s1 - m)
    p2 = jnp.exp2(s2 - m)
    l = jnp.sum(p1, axis=-1, keepdims=True) + jnp.sum(p2, axis=-1, keepdims=True)
    o = _dot(p1.astype(BF16), vl_ref[0]) + _dot(p2.astype(BF16), vc_ref[0])
    o_ref[0] = (o / l).astype(BF16)


def _outproj_kernel(x_ref, mod_ref, o_ref, sg_ref, wout_ref, out_ref):
    y = o_ref[0].astype(F32) * sg_ref[0].astype(F32)
    out_ref[0] = x_ref[0] + mod_ref[0][2:3] * _dot(y.astype(BF16), wout_ref[...])


def _mla_layer(x, cx, mod, mod_c, ng, w_in, q_norm, kv_norm, w_uq, w_ukv, nope_g, rope_g, w_out):
    bsz, seq, _ = x.shape
    cseq = cx.shape[1]
    kvc = KV_RANK + ROPE
    w_kv_in = jnp.pad(w_in[:, :kvc], ((0, 0), (0, 128 - ROPE)))
    w_in_p = jnp.concatenate([w_kv_in, w_in[:, kvc:]], axis=1).astype(BF16)
    w_kv_in = w_kv_in.astype(BF16)
    w_uq_p = jnp.pad(w_uq.reshape(Q_RANK, HEADS, NOPE + ROPE),
                     ((0, 0), (0, 0), (0, HEAD_QK - NOPE - ROPE))).reshape(Q_RANK, HEADS * HEAD_QK).astype(BF16)
    w_ukv_b = w_ukv.astype(BF16)
    pad_g = lambda g: jnp.pad(g, (0, 128 - ROPE)).reshape(1, 128)
    qng, kng = nope_g[0].reshape(1, NOPE), nope_g[1].reshape(1, NOPE)
    qrg, krg = pad_g(rope_g[0]), pad_g(rope_g[1])
    cos, sup, sdn = _rope_tables(seq)

    tr = 256
    win_cols = w_in_p.shape[1]
    row = lambda w: pl.BlockSpec((1, tr, w), lambda b, j: (b, j, 0))
    modspec = pl.BlockSpec((1, 3, D), lambda b, j: (b, 0, 0))
    tab = pl.BlockSpec((tr, 128), lambda b, j: (j, 0))
    q, k, v, sg = pl.pallas_call(
        _mla_proj_kernel,
        out_shape=(jax.ShapeDtypeStruct((bsz, seq, HEADS * HEAD_QK), BF16),
                   jax.ShapeDtypeStruct((bsz, seq, HEADS * HEAD_QK), BF16),
                   jax.ShapeDtypeStruct((bsz, seq, HEADS * HEAD_V), BF16),
                   jax.ShapeDtypeStruct((bsz, seq, D), BF16)),
        grid=(bsz, seq // tr),
        in_specs=[row(D), modspec, _const_spec((1, D)),
                  _const_spec((D, win_cols)),
                  _const_spec((1, Q_RANK)), _const_spec((1, KV_RANK)),
                  _const_spec((Q_RANK, HEADS * HEAD_QK)), _const_spec((KV_RANK, HEADS * HEAD_QK)),
                  _const_spec((1, NOPE)), _const_spec((1, 128)), _const_spec((1, NOPE)), _const_spec((1, 128)),
                  tab, tab, tab],
        out_specs=(row(HEADS * HEAD_QK), row(HEADS * HEAD_QK), row(HEADS * HEAD_V), row(D)),
        compiler_params=_params(2),
        name="mla_proj",
    )(x, mod, ng, w_in_p, q_norm.reshape(1, -1), kv_norm.reshape(1, -1), w_uq_p, w_ukv_b,
      qng, qrg, kng, krg, cos, sup, sdn)

    trc = min(tr, cseq)
    rowc = lambda w: pl.BlockSpec((1, trc, w), lambda b, j: (b, j, 0))
    kc, vc = pl.pallas_call(
        _mla_ctx_kernel,
        out_shape=(jax.ShapeDtypeStruct((bsz, cseq, HEADS * HEAD_QK), BF16),
                   jax.ShapeDtypeStruct((bsz, cseq, HEADS * HEAD_V), BF16)),
        grid=(bsz, cseq // trc),
        in_specs=[rowc(D), modspec, _const_spec((1, D)),
                  _const_spec((D, KV_RANK + 128)),
                  _const_spec((1, KV_RANK)),
                  _const_spec((KV_RANK, HEADS * HEAD_QK)),
                  _const_spec((1, NOPE)), _const_spec((1, 128))],
        out_specs=(rowc(HEADS * HEAD_QK), rowc(HEADS * HEAD_V)),
        compiler_params=_params(2),
        name="mla_ctx_proj",
    )(cx, mod_c, ng, w_kv_in, kv_norm.reshape(1, -1), w_ukv_b, kng, krg)

    tq = ATTN_Q_TILE
    o = pl.pallas_call(
        _attn_kernel,
        out_shape=jax.ShapeDtypeStruct((bsz, seq, HEADS * HEAD_V), BF16),
        grid=(bsz, HEADS, seq // tq),
        in_specs=[pl.BlockSpec((1, tq, HEAD_QK), lambda b, h, i: (b, i, h)),
                  pl.BlockSpec((1, seq, HEAD_QK), lambda b, h, i: (b, 0, h)),
                  pl.BlockSpec((1, seq, HEAD_V), lambda b, h, i: (b, 0, h)),
                  pl.BlockSpec((1, cseq, HEAD_QK), lambda b, h, i: (b, 0, h)),
                  pl.BlockSpec((1, cseq, HEAD_V), lambda b, h, i: (b, 0, h))],
        out_specs=pl.BlockSpec((1, tq, HEAD_V), lambda b, h, i: (b, i, h)),
        compiler_params=_params(3),
        name="mla_attention",
    )(q, k, v, kc, vc)

    tl = min(SEQ_TILE, seq)
    main = lambda dt: pl.BlockSpec((1, tl, D), lambda b, j: (b, j, 0))
    return pl.pallas_call(
        _outproj_kernel,
        out_shape=jax.ShapeDtypeStruct(x.shape, F32),
        grid=(bsz, seq // tl),
        in_specs=[main(F32), modspec, main(BF16), main(BF16), _const_spec((D, D))],
        out_specs=main(F32),
        compiler_params=_params(2),
        name="mla_outproj",
    )(x, mod, o, sg, w_out.astype(BF16))


def kernel(x, c, ctx, c_ctx, norm_g, w_mod, b_mod, cv_w_in, cv_dw, cv_db, cv_ln_g, cv_ln_b, cv_w_out, pl_w_in, pl_w_grp, pl_scale, pl_w_out, ml_w_in, ml_q_norm, ml_kv_norm, ml_w_uq, ml_w_ukv, ml_nope_norm, ml_rope_norm, ml_w_out, ch_w_in, ch_ln_g, ch_ln_b, ch_w_s, ch_b_s, ch_w_out):
    bsz = x.shape[0]
    rows = -(-(bsz + 1) // 8) * 8
    cvec = jnp.concatenate([c, c_ctx[None, :], jnp.zeros((rows - bsz - 1, D), F32)], axis=0)
    mods = _modulation(cvec, w_mod, b_mod)
    row2 = lambda a: a.reshape(1, -1)
    cx = ctx
    for i in range(DEPTH):
        kind, j = i % 4, i // 4
        ctx_out = any(k % 4 == 2 for k in range(i + 1, DEPTH))
        ctx_in = ctx_out or kind == 2
        mod = mods[i, :bsz].reshape(bsz, 3, D)
        mod_c = jnp.broadcast_to(mods[i, bsz].reshape(1, 3, D), (bsz, 3, D))
        ng = row2(norm_g[i])
        if kind == 0:
            args = (cv_w_in[j].astype(BF16), cv_dw[j], row2(cv_db[j]), row2(cv_ln_g[j]), row2(cv_ln_b[j]),
                    cv_w_out[j].astype(BF16))
            layer = _conv_layer
        elif kind == 1:
            args = (pl_w_in[j].astype(BF16), pl_w_grp[j].astype(BF16), row2(pl_scale[j]), pl_w_out[j].astype(BF16))
            layer = _pool_layer
        elif kind == 3:
            bias = jnp.repeat(ch_b_s[j], D // CHUNK_GROUPS, axis=1)
            args = (ch_w_in[j].astype(BF16), row2(ch_ln_g[j]), row2(ch_ln_b[j]), ch_w_s[j].astype(BF16), bias,
                    ch_w_out[j].astype(BF16))
            layer = _chunk_layer
        if kind == 2:
            x = _mla_layer(x, cx, mod, mod_c, ng, ml_w_in[j], ml_q_norm[j], ml_kv_norm[j], ml_w_uq[j],
                           ml_w_ukv[j], ml_nope_norm[j], ml_rope_norm[j], ml_w_out[j])
        else:
            if ctx_out:
                cx = layer(cx, mod_c, ng, *args)
            x = layer(x, mod, ng, *args)
        del ctx_in
    return x
```

```python
import functools

import numpy as np
import jax
import jax.numpy as jnp
from jax import lax
from jax.experimental import pallas as pl
from jax.experimental.pallas import tpu as pltpu

D = 1024
DEPTH = 4
EPS = 1e-6
GRID_W = 64
CONV_WIDTH = 31
CONV_HALO = 16
CONV_ROW_BLOCK = 128
CONV_LANE_BLOCK = 256
POOL_WINDOWS = (2, 4, 8, 16)
POOL_GROUP = D // len(POOL_WINDOWS)
POOL_HALO = 8
HEADS = 8
NOPE = 128
ROPE = 64
HEAD_V = 128
HEAD_QK = 256
Q_RANK = 384
KV_RANK = 256
ROPE_THETA = 10000.0
SCORE_SCALE = (NOPE + ROPE) ** -0.5
LOG2E = 1.4426950408889634
CHUNK = 128
CHUNK_GROUPS = 8
SEQ_TILE = 512
ATTN_Q_TILE = 2048
ATTN_Q_SUB = 256
VMEM_LIMIT = 56 * 1024 * 1024

F32 = jnp.float32
BF16 = jnp.bfloat16


def _sigmoid(x):
    return 0.5 * jnp.tanh(0.5 * x) + 0.5


def _silu(x):
    return x * _sigmoid(x)


def _dot(a, b):
    return jnp.dot(a, b, preferred_element_type=F32)


def _dot_nt(a, b):
    return lax.dot_general(a, b, (((1,), (1,)), ((), ())), preferred_element_type=F32)


def _rms(x, g, n=None):
    n = x.shape[-1] if n is None else n
    ms = jnp.sum(x * x, axis=-1, keepdims=True) * (1.0 / n)
    return (x * lax.rsqrt(ms + EPS)) * g


def _layernorm(x, g, b):
    mu = jnp.mean(x, axis=-1, keepdims=True)
    xc = x - mu
    var = jnp.mean(xc * xc, axis=-1, keepdims=True)
    return (xc * lax.rsqrt(var + EPS)) * g + b


def _modulated(x, ng, mod):
    return _rms(x, ng) * (1.0 + mod[1:2]) + mod[0:1]


def _const_spec(shape):
    nd = len(shape)
    return pl.BlockSpec(shape, lambda *_: (0,) * nd)


def _params(n_axes):
    return pltpu.CompilerParams(dimension_semantics=("arbitrary",) * n_axes,
                                vmem_limit_bytes=VMEM_LIMIT)


def _mod_kernel(cv_ref, w_ref, b_ref, o_ref):
    s = _silu(cv_ref[...])
    o_ref[0] = _dot(s.astype(BF16), w_ref[0].astype(BF16)) + b_ref[0]


def _modulation(cvec, w_mod, b_mod):
    rows = cvec.shape[0]
    nt = 3
    return pl.pallas_call(
        _mod_kernel,
        out_shape=jax.ShapeDtypeStruct((DEPTH, rows, 3 * D), F32),
        grid=(DEPTH, nt),
        in_specs=[_const_spec((rows, D)),
                  pl.BlockSpec((1, D, D), lambda i, n: (i, 0, n)),
                  pl.BlockSpec((1, 1, D), lambda i, n: (i, 0, n))],
        out_specs=pl.BlockSpec((1, rows, D), lambda i, n: (i, 0, n)),
        compiler_params=_params(2),
        name="modulation",
    )(cvec, w_mod, b_mod.reshape(DEPTH, 1, 3 * D))


def _halo_specs(tl, seq, halo):
    per = tl // halo
    last = seq // halo - 1
    prev = pl.BlockSpec((1, halo, D), lambda b, j: (b, jnp.maximum(j * per - 1, 0), 0))
    main = pl.BlockSpec((1, tl, D), lambda b, j: (b, j, 0))
    nxt = pl.BlockSpec((1, halo, D), lambda b, j: (b, jnp.minimum((j + 1) * per, last), 0))
    return prev, main, nxt


def _row_valid(tl, seq, halo):
    e = lax.broadcasted_iota(jnp.int32, (tl + 2 * halo, 1), 0)
    t = pl.program_id(1) * tl - halo + e
    return (t >= 0) & (t < seq)


def _conv_kernel(xp_ref, xm_ref, xn_ref, mod_ref, ng_ref, win_ref, dw_ref, db_ref,
                 lng_ref, lnb_ref, wout_ref, o_ref, ybuf, zbuf, ysh, *, tl, seq):
    halo = CONV_HALO
    mod = mod_ref[0]
    xm = xm_ref[0]
    x_ext = jnp.concatenate([xp_ref[0], xm, xn_ref[0]], axis=0)
    h = _modulated(x_ext, ng_ref[...], mod).astype(BF16)
    a = _dot(h, win_ref[:, 0:D])
    b = _dot(h, win_ref[:, D:2 * D])
    y = a * _sigmoid(b)
    ybuf[...] = jnp.where(_row_valid(tl, seq, halo), y, 0.0)
    sgate = _silu(_dot(h[halo:halo + tl], win_ref[:, 2 * D:3 * D]))

    base = halo - CONV_WIDTH // 2
    span = tl + 8 * ((base + CONV_WIDTH - 1) // 8)
    rb, lb = CONV_ROW_BLOCK, CONV_LANE_BLOCK
    for c0 in range(0, D, lb):
        for r in range(1, 8):
            ysh[r - 1] = ybuf[r:r + span, c0:c0 + lb]
        for r0 in range(0, tl, rb):
            for cc in range(0, lb, 128):
                acc = None
                for k in range(CONV_WIDTH):
                    a, r = divmod(base + k, 8)
                    if r == 0:
                        src = ybuf[r0 + 8 * a:r0 + 8 * a + rb, c0 + cc:c0 + cc + 128]
                    else:
                        src = ysh[r - 1, r0 + 8 * a:r0 + 8 * a + rb, cc:cc + 128]
                    term = dw_ref[k:k + 1, c0 + cc:c0 + cc + 128] * src
                    acc = term if acc is None else acc + term
                zbuf[r0:r0 + rb, c0 + cc:c0 + cc + 128] = acc
    z = zbuf[...] + db_ref[...]
    u = _silu(_layernorm(z, lng_ref[...], lnb_ref[...])) * sgate
    o_ref[0] = xm + mod[2:3] * _dot(u.astype(BF16), wout_ref[...])


def _conv_layer(x, mod, ng, w_in, dw, db, ln_g, ln_b, w_out):
    bsz, seq, _ = x.shape
    tl = min(SEQ_TILE, seq)
    prev, main, nxt = _halo_specs(tl, seq, CONV_HALO)
    return pl.pallas_call(
        functools.partial(_conv_kernel, tl=tl, seq=seq),
        out_shape=jax.ShapeDtypeStruct(x.shape, F32),
        grid=(bsz, seq // tl),
        in_specs=[prev, main, nxt,
                  pl.BlockSpec((1, 3, D), lambda b, j: (b, 0, 0)),
                  _const_spec((1, D)),
                  _const_spec((D, 3 * D)),
                  _const_spec((CONV_WIDTH, D)),
                  _const_spec((1, D)), _const_spec((1, D)), _const_spec((1, D)),
                  _const_spec((D, D))],
        out_specs=main,
        scratch_shapes=[pltpu.VMEM((tl + 2 * CONV_HALO, D), F32),
                        pltpu.VMEM((tl, D), F32),
                        pltpu.VMEM((7, tl + 2 * CONV_HALO - 8, CONV_LANE_BLOCK), F32)],
        compiler_params=_params(2),
        name="conv_layer",
    )(x, x, x, mod, ng, w_in, dw, db, ln_g, ln_b, w_out)


def _pool_kernel(xp_ref, xm_ref, xn_ref, mod_ref, ng_ref, win_ref, wgrp_ref, scale_ref,
                 wout_ref, o_ref, vbuf, *, tl, seq):
    halo = POOL_HALO
    mod = mod_ref[0]
    xm = xm_ref[0]
    x_ext = jnp.concatenate([xp_ref[0], xm, xn_ref[0]], axis=0)
    h = _modulated(x_ext, ng_ref[...], mod).astype(BF16)
    v = _dot(h, win_ref[:, 0:D])
    vbuf[...] = jnp.where(_row_valid(tl, seq, halo), v, 0.0)
    sgate = _silu(_dot(h[halo:halo + tl], win_ref[:, D:2 * D]))

    t = pl.program_id(1) * tl + lax.broadcasted_iota(jnp.int32, (tl, 1), 0)
    ys = []
    for k, w in enumerate(POOL_WINDOWS):
        c0 = k * POOL_GROUP
        tot = jnp.zeros((tl, POOL_GROUP), F32)
        for off in range(-(w // 2), w - w // 2):
            tot = tot + vbuf[halo + off:halo + off + tl, c0:c0 + POOL_GROUP]
        cnt = (jnp.minimum(t + (w - w // 2), seq) - jnp.maximum(t - w // 2, 0)).astype(F32)
        pooled = tot / cnt - vbuf[halo:halo + tl, c0:c0 + POOL_GROUP]
        ys.append(_dot(pooled.astype(BF16), wgrp_ref[k]))
    y = jnp.concatenate(ys, axis=1) * scale_ref[...]
    o_ref[0] = xm + mod[2:3] * _dot((y * sgate).astype(BF16), wout_ref[...])


def _pool_layer(x, mod, ng, w_in, w_grp, scale, w_out):
    bsz, seq, _ = x.shape
    tl = min(SEQ_TILE, seq)
    prev, main, nxt = _halo_specs(tl, seq, POOL_HALO)
    return pl.pallas_call(
        functools.partial(_pool_kernel, tl=tl, seq=seq),
        out_shape=jax.ShapeDtypeStruct(x.shape, F32),
        grid=(bsz, seq // tl),
        in_specs=[prev, main, nxt,
                  pl.BlockSpec((1, 3, D), lambda b, j: (b, 0, 0)),
                  _const_spec((1, D)),
                  _const_spec((D, 2 * D)),
                  _const_spec((len(POOL_WINDOWS), POOL_GROUP, POOL_GROUP)),
                  _const_spec((1, D)),
                  _const_spec((D, D))],
        out_specs=main,
        scratch_shapes=[pltpu.VMEM((tl + 2 * POOL_HALO, D), F32)],
        compiler_params=_params(2),
        name="pool_layer",
    )(x, x, x, mod, ng, w_in, w_grp, scale, w_out)


def _chunk_kernel(x_ref, mod_ref, ng_ref, win_ref, lng_ref, lnb_ref, ws_ref, bs_ref,
                  wout_ref, o_ref, *, tl):
    mod = mod_ref[0]
    x = x_ref[0]
    h = _modulated(x, ng_ref[...], mod).astype(BF16)
    u = _dot(h, win_ref[:, 0:D])
    v = _layernorm(_dot(h, win_ref[:, D:2 * D]), lng_ref[...], lnb_ref[...]).astype(BF16)
    sgate = _silu(_dot(h, win_ref[:, 2 * D:3 * D]))
    nch = tl // CHUNK
    gc = D // CHUNK_GROUPS
    cols = []
    for g in range(CHUNK_GROUPS):
        vg = jnp.concatenate([v[n * CHUNK:(n + 1) * CHUNK, g * gc:(g + 1) * gc] for n in range(nch)], axis=1)
        sg = _dot(ws_ref[g], vg)
        cols.append(jnp.concatenate([sg[:, n * gc:(n + 1) * gc] for n in range(nch)], axis=0))
    s = jnp.concatenate(cols, axis=1) + jnp.concatenate([bs_ref[...]] * nch, axis=0)
    y = u * s * sgate
    o_ref[0] = x + mod[2:3] * _dot(y.astype(BF16), wout_ref[...])


def _chunk_layer(x, mod, ng, w_in, ln_g, ln_b, w_s, b_s, w_out):
    bsz, seq, _ = x.shape
    tl = min(SEQ_TILE, seq)
    main = pl.BlockSpec((1, tl, D), lambda b, j: (b, j, 0))
    return pl.pallas_call(
        functools.partial(_chunk_kernel, tl=tl),
        out_shape=jax.ShapeDtypeStruct(x.shape, F32),
        grid=(bsz, seq // tl),
        in_specs=[main,
                  pl.BlockSpec((1, 3, D), lambda b, j: (b, 0, 0)),
                  _const_spec((1, D)),
                  _const_spec((D, 3 * D)),
                  _const_spec((1, D)), _const_spec((1, D)),
                  _const_spec((CHUNK_GROUPS, CHUNK, CHUNK)),
                  _const_spec((CHUNK, D)),
                  _const_spec((D, D))],
        out_specs=main,
        compiler_params=_params(2),
        name="chunk_layer",
    )(x, mod, ng, w_in, ln_g, ln_b, w_s, b_s, w_out)


def _rope_tables(seq):
    rows = seq // GRID_W
    row_id = np.repeat(np.arange(rows), GRID_W).astype(np.float32)
    col_id = np.tile(np.arange(GRID_W), rows).astype(np.float32)
    axis_dim = ROPE // 2
    freqs = (np.float32(ROPE_THETA) ** (-np.arange(0, axis_dim, 2, dtype=np.float32) / axis_dim)).astype(np.float32)
    ar = row_id[:, None] * freqs
    ac = col_id[:, None] * freqs
    q = axis_dim // 2
    zeros = np.zeros((seq, q), np.float32)
    pad = np.zeros((seq, 128 - ROPE), np.float32)
    cos = np.concatenate([np.cos(ar), np.cos(ar), np.cos(ac), np.cos(ac), pad], axis=1)
    sin_up = np.concatenate([-np.sin(ar), zeros, -np.sin(ac), zeros, pad], axis=1)
    sin_dn = np.concatenate([zeros, np.sin(ar), zeros, np.sin(ac), pad], axis=1)
    return (jnp.asarray(cos, F32), jnp.asarray(sin_up, F32), jnp.asarray(sin_dn, F32))


def _rope(x, cos, sin_up, sin_dn):
    quarter = ROPE // 4
    up = pltpu.roll(x, 128 - quarter, axis=1)
    dn = pltpu.roll(x, quarter, axis=1)
    return x * cos + up * sin_up + dn * sin_dn


def _mla_keys(ckv, kr, kvn_ref, wukv_ref, kng_ref, krg_ref, tabs, k_ref, v_ref):
    kv = _dot(_rms(ckv, kvn_ref[...]).astype(BF16), wukv_ref[...])
    krn = _rms(kr, krg_ref[...], n=ROPE)
    if tabs is not None:
        krn = _rope(krn, *tabs)
    krn = krn.astype(BF16)
    kng = kng_ref[...]
    for hd in range(HEADS):
        c0 = hd * HEAD_QK
        k_ref[0, :, c0:c0 + NOPE] = _rms(kv[:, c0:c0 + NOPE], kng).astype(BF16)
        k_ref[0, :, c0 + NOPE:c0 + HEAD_QK] = krn
        v_ref[0, :, hd * HEAD_V:(hd + 1) * HEAD_V] = kv[:, c0 + NOPE:c0 + HEAD_QK].astype(BF16)


def _mla_proj_kernel(x_ref, mod_ref, ng_ref, win_ref, qn_ref, kvn_ref, wuq_ref, wukv_ref,
                     qng_ref, qrg_ref, kng_ref, krg_ref, cos_ref, sup_ref, sdn_ref,
                     q_ref, k_ref, v_ref, sg_ref):
    h = _modulated(x_ref[0], ng_ref[...], mod_ref[0]).astype(BF16)
    p = _dot(h, win_ref[...])
    tabs = (cos_ref[...], sup_ref[...], sdn_ref[...])
    _mla_keys(p[:, 0:KV_RANK], p[:, KV_RANK:KV_RANK + 128], kvn_ref, wukv_ref, kng_ref, krg_ref,
              tabs, k_ref, v_ref)
    c1 = KV_RANK + 128
    q = _dot(_rms(p[:, c1:c1 + Q_RANK], qn_ref[...]).astype(BF16), wuq_ref[...])
    qng = qng_ref[...]
    qrg = qrg_ref[...]
    qs = SCORE_SCALE * LOG2E
    for hd in range(HEADS):
        c0 = hd * HEAD_QK
        q_ref[0, :, c0:c0 + NOPE] = (_rms(q[:, c0:c0 + NOPE], qng) * qs).astype(BF16)
        qr = _rope(_rms(q[:, c0 + NOPE:c0 + HEAD_QK], qrg, n=ROPE), *tabs)
        q_ref[0, :, c0 + NOPE:c0 + HEAD_QK] = (qr * qs).astype(BF16)
    sg_ref[0] = _silu(p[:, c1 + Q_RANK:c1 + Q_RANK + D]).astype(BF16)


def _mla_ctx_kernel(x_ref, mod_ref, ng_ref, win_ref, kvn_ref, wukv_ref, kng_ref, krg_ref,
                    k_ref, v_ref):
    h = _modulated(x_ref[0], ng_ref[...], mod_ref[0]).astype(BF16)
    p = _dot(h, win_ref[...])
    _mla_keys(p[:, 0:KV_RANK], p[:, KV_RANK:KV_RANK + 128], kvn_ref, wukv_ref, kng_ref, krg_ref,
              None, k_ref, v_ref)


def _attn_kernel(q_ref, kl_ref, vl_ref, kc_ref, vc_ref, o_ref, *, sub):
    for r0 in range(0, q_ref.shape[1], sub):
        q = q_ref[0, r0:r0 + sub, :]
        s1 = _dot_nt(q, kl_ref[0])
        s2 = _dot_nt(q, kc_ref[0])
        m = jnp.maximum(jnp.max(s1, axis=-1, keepdims=True), jnp.max(s2, axis=-1, keepdims=True))
        p1 = jnp.exp2(s1 - m)
        p2 = jnp.exp2(s2 - m)
        l = jnp.sum(p1, axis=-1, keepdims=True) + jnp.sum(p2, axis=-1, keepdims=True)
        o = _dot(p1.astype(BF16), vl_ref[0]) + _dot(p2.astype(BF16), vc_ref[0])
        o_ref[0, r0:r0 + sub, :] = (o / l).astype(BF16)


def _attention(q, k, v, kc, vc):
    bsz, seq, _ = q.shape
    cseq = kc.shape[1]
    tq = min(ATTN_Q_TILE, seq)
    return pl.pallas_call(
        functools.partial(_attn_kernel, sub=min(ATTN_Q_SUB, tq)),
        out_shape=jax.ShapeDtypeStruct((bsz, seq, HEADS * HEAD_V), BF16),
        grid=(bsz, HEADS, seq // tq),
        in_specs=[pl.BlockSpec((1, tq, HEAD_QK), lambda b, h, i: (b, i, h)),
                  pl.BlockSpec((1, seq, HEAD_QK), lambda b, h, i: (b, 0, h)),
                  pl.BlockSpec((1, seq, HEAD_V), lambda b, h, i: (b, 0, h)),
                  pl.BlockSpec((1, cseq, HEAD_QK), lambda b, h, i: (b, 0, h)),
                  pl.BlockSpec((1, cseq, HEAD_V), lambda b, h, i: (b, 0, h))],
        out_specs=pl.BlockSpec((1, tq, HEAD_V), lambda b, h, i: (b, i, h)),
        compiler_params=_params(3),
        name="mla_attention",
    )(q, k, v, kc, vc)


def _outproj_kernel(x_ref, mod_ref, o_ref, sg_ref, wout_ref, out_ref):
    y = o_ref[0].astype(F32) * sg_ref[0].astype(F32)
    out_ref[0] = x_ref[0] + mod_ref[0][2:3] * _dot(y.astype(BF16), wout_ref[...])


def _mla_layer(x, cx, mod, mod_c, ng, w_in, q_norm, kv_norm, w_uq, w_ukv, nope_g, rope_g, w_out):
    bsz, seq, _ = x.shape
    cseq = cx.shape[1]
    kvc = KV_RANK + ROPE
    w_kv_in = jnp.pad(w_in[:, :kvc], ((0, 0), (0, 128 - ROPE)))
    w_in_p = jnp.concatenate([w_kv_in, w_in[:, kvc:]], axis=1).astype(BF16)
    w_kv_in = w_kv_in.astype(BF16)
    w_uq_p = jnp.pad(w_uq.reshape(Q_RANK, HEADS, NOPE + ROPE),
                     ((0, 0), (0, 0), (0, HEAD_QK - NOPE - ROPE))).reshape(Q_RANK, HEADS * HEAD_QK).astype(BF16)
    w_ukv_b = w_ukv.astype(BF16)
    pad_g = lambda g: jnp.pad(g, (0, 128 - ROPE)).reshape(1, 128)
    qng, kng = nope_g[0].reshape(1, NOPE), nope_g[1].reshape(1, NOPE)
    qrg, krg = pad_g(rope_g[0]), pad_g(rope_g[1])
    cos, sup, sdn = _rope_tables(seq)

    tr = 256
    win_cols = w_in_p.shape[1]
    row = lambda w: pl.BlockSpec((1, tr, w), lambda b, j: (b, j, 0))
    modspec = pl.BlockSpec((1, 3, D), lambda b, j: (b, 0, 0))
    tab = pl.BlockSpec((tr, 128), lambda b, j: (j, 0))
    q, k, v, sg = pl.pallas_call(
        _mla_proj_kernel,
        out_shape=(jax.ShapeDtypeStruct((bsz, seq, HEADS * HEAD_QK), BF16),
                   jax.ShapeDtypeStruct((bsz, seq, HEADS * HEAD_QK), BF16),
                   jax.ShapeDtypeStruct((bsz, seq, HEADS * HEAD_V), BF16),
                   jax.ShapeDtypeStruct((bsz, seq, D), BF16)),
        grid=(bsz, seq // tr),
        in_specs=[row(D), modspec, _const_spec((1, D)),
                  _const_spec((D, win_cols)),
                  _const_spec((1, Q_RANK)), _const_spec((1, KV_RANK)),
                  _const_spec((Q_RANK, HEADS * HEAD_QK)), _const_spec((KV_RANK, HEADS * HEAD_QK)),
                  _const_spec((1, NOPE)), _const_spec((1, 128)), _const_spec((1, NOPE)), _const_spec((1, 128)),
                  tab, tab, tab],
        out_specs=(row(HEADS * HEAD_QK), row(HEADS * HEAD_QK), row(HEADS * HEAD_V), row(D)),
        compiler_params=_params(2),
        name="mla_proj",
    )(x, mod, ng, w_in_p, q_norm.reshape(1, -1), kv_norm.reshape(1, -1), w_uq_p, w_ukv_b,
      qng, qrg, kng, krg, cos, sup, sdn)

    trc = min(tr, cseq)
    rowc = lambda w: pl.BlockSpec((1, trc, w), lambda b, j: (b, j, 0))
    kc, vc = pl.pallas_call(
        _mla_ctx_kernel,
        out_shape=(jax.ShapeDtypeStruct((bsz, cseq, HEADS * HEAD_QK), BF16),
                   jax.ShapeDtypeStruct((bsz, cseq, HEADS * HEAD_V), BF16)),
        grid=(bsz, cseq // trc),
        in_specs=[rowc(D), modspec, _const_spec((1, D)),
                  _const_spec((D, KV_RANK + 128)),
                  _const_spec((1, KV_RANK)),
                  _const_spec((KV_RANK, HEADS * HEAD_QK)),
                  _const_spec((1, NOPE)), _const_spec((1, 128))],
        out_specs=(rowc(HEADS * HEAD_QK), rowc(HEADS * HEAD_V)),
        compiler_params=_params(2),
        name="mla_ctx_proj",
    )(cx, mod_c, ng, w_kv_in, kv_norm.reshape(1, -1), w_ukv_b, kng, krg)

    o = _attention(q, k, v, kc, vc)

    tl = min(SEQ_TILE, seq)
    main = lambda dt: pl.BlockSpec((1, tl, D), lambda b, j: (b, j, 0))
    return pl.pallas_call(
        _outproj_kernel,
        out_shape=jax.ShapeDtypeStruct(x.shape, F32),
        grid=(bsz, seq // tl),
        in_specs=[main(F32), modspec, main(BF16), main(BF16), _const_spec((D, D))],
        out_specs=main(F32),
        compiler_params=_params(2),
        name="mla_outproj",
    )(x, mod, o, sg, w_out.astype(BF16))


def kernel(x, c, ctx, c_ctx, norm_g, w_mod, b_mod, cv_w_in, cv_dw, cv_db, cv_ln_g, cv_ln_b, cv_w_out, pl_w_in, pl_w_grp, pl_scale, pl_w_out, ml_w_in, ml_q_norm, ml_kv_norm, ml_w_uq, ml_w_ukv, ml_nope_norm, ml_rope_norm, ml_w_out, ch_w_in, ch_ln_g, ch_ln_b, ch_w_s, ch_b_s, ch_w_out):
    bsz = x.shape[0]
    rows = -(-(bsz + 1) // 8) * 8
    cvec = jnp.concatenate([c, c_ctx[None, :], jnp.zeros((rows - bsz - 1, D), F32)], axis=0)
    mods = _modulation(cvec, w_mod, b_mod)
    row2 = lambda a: a.reshape(1, -1)
    cx = ctx
    for i in range(DEPTH):
        kind, j = i % 4, i // 4
        ctx_out = any(k % 4 == 2 for k in range(i + 1, DEPTH))
        ctx_in = ctx_out or kind == 2
        mod = mods[i, :bsz].reshape(bsz, 3, D)
        mod_c = jnp.broadcast_to(mods[i, bsz].reshape(1, 3, D), (bsz, 3, D))
        ng = row2(norm_g[i])
        if kind == 0:
            args = (cv_w_in[j].astype(BF16), cv_dw[j], row2(cv_db[j]), row2(cv_ln_g[j]), row2(cv_ln_b[j]),
                    cv_w_out[j].astype(BF16))
            layer = _conv_layer
        elif kind == 1:
            args = (pl_w_in[j].astype(BF16), pl_w_grp[j].astype(BF16), row2(pl_scale[j]), pl_w_out[j].astype(BF16))
            layer = _pool_layer
        elif kind == 3:
            bias = jnp.repeat(ch_b_s[j], D // CHUNK_GROUPS, axis=1)
            args = (ch_w_in[j].astype(BF16), row2(ch_ln_g[j]), row2(ch_ln_b[j]), ch_w_s[j].astype(BF16), bias,
                    ch_w_out[j].astype(BF16))
            layer = _chunk_layer
        if kind == 2:
            x = _mla_layer(x, cx, mod, mod_c, ng, ml_w_in[j], ml_q_norm[j], ml_kv_norm[j], ml_w_uq[j],
                           ml_w_ukv[j], ml_nope_norm[j], ml_rope_norm[j], ml_w_out[j])
        else:
            if ctx_out:
                cx = layer(cx, mod_c, ng, *args)
            x = layer(x, mod, ng, *args)
        del ctx_in
    return x
```

```python
import functools

import numpy as np
import jax
import jax.numpy as jnp
from jax import lax
from jax.experimental import pallas as pl
from jax.experimental.pallas import tpu as pltpu

D = 1024
DEPTH = 4
EPS = 1e-6
GRID_W = 64
CONV_WIDTH = 31
CONV_HALO = 16
CONV_ROW_STRIDE = 4
CONV_LANE_BLOCK = 256
CONV_GROUP_ROWS = 64
POOL_WINDOWS = (2, 4, 8, 16)
POOL_GROUP = D // len(POOL_WINDOWS)
POOL_HALO = 8
POOL_LOG_FROM = 2
HEADS = 8
NOPE = 128
ROPE = 64
HEAD_V = 128
HEAD_QK = 256
Q_RANK = 384
KV_RANK = 256
ROPE_THETA = 10000.0
SCORE_SCALE = (NOPE + ROPE) ** -0.5
LOG2E = 1.4426950408889634
CHUNK = 128
CHUNK_GROUPS = 8
SEQ_TILE = 512
MLA_ROW_TILE = 256
ATTN_Q_TILE = 2048
ATTN_Q_SUB = 256
VMEM_LIMIT = 56 * 1024 * 1024

F32 = jnp.float32
BF16 = jnp.bfloat16


def _silu_half(hx):
    return hx + hx * jnp.tanh(hx)


def _glu_half(ha, hb):
    return ha + ha * jnp.tanh(hb)


def _silu(x):
    return _silu_half(0.5 * x)


def _dot(a, b):
    return jnp.dot(a, b, preferred_element_type=F32)


def _dot_nt(a, b):
    return lax.dot_general(a, b, (((1,), (1,)), ((), ())), preferred_element_type=F32)


def _rms(x, g, n=None):
    n = x.shape[-1] if n is None else n
    ms = jnp.sum(x * x, axis=-1, keepdims=True) * (1.0 / n)
    return (x * lax.rsqrt(ms + EPS)) * g


def _layernorm(x, g, b):
    mu = jnp.mean(x, axis=-1, keepdims=True)
    xc = x - mu
    var = jnp.mean(xc * xc, axis=-1, keepdims=True)
    return (xc * lax.rsqrt(var + EPS)) * g + b


def _modulated(x, ng, mod):
    return _rms(x, ng) * (1.0 + mod[1:2]) + mod[0:1]


def _const_spec(shape):
    nd = len(shape)
    return pl.BlockSpec(shape, lambda *_: (0,) * nd)


def _params(n_axes):
    return pltpu.CompilerParams(dimension_semantics=("arbitrary",) * n_axes,
                                vmem_limit_bytes=VMEM_LIMIT)


def _mod_kernel(cv_ref, w_ref, b_ref, o_ref):
    s = _silu(cv_ref[...])
    o_ref[0] = _dot(s.astype(BF16), w_ref[0].astype(BF16)) + b_ref[0]


def _modulation(cvec, w_mod, b_mod):
    rows = cvec.shape[0]
    nt = 3
    return pl.pallas_call(
        _mod_kernel,
        out_shape=jax.ShapeDtypeStruct((DEPTH, rows, 3 * D), F32),
        grid=(DEPTH, nt),
        in_specs=[_const_spec((rows, D)),
                  pl.BlockSpec((1, D, D), lambda i, n: (i, 0, n)),
                  pl.BlockSpec((1, 1, D), lambda i, n: (i, 0, n))],
        out_specs=pl.BlockSpec((1, rows, D), lambda i, n: (i, 0, n)),
        compiler_params=_params(2),
        name="modulation",
    )(cvec, w_mod, b_mod.reshape(DEPTH, 1, 3 * D))


def _halo_specs(tl, seq, halo):
    per = tl // halo
    last = seq // halo - 1
    prev = pl.BlockSpec((1, halo, D), lambda b, j: (b, jnp.maximum(j * per - 1, 0), 0))
    main = pl.BlockSpec((1, tl, D), lambda b, j: (b, j, 0))
    nxt = pl.BlockSpec((1, halo, D), lambda b, j: (b, jnp.minimum((j + 1) * per, last), 0))
    return prev, main, nxt


def _row_valid(tl, seq, halo):
    e = lax.broadcasted_iota(jnp.int32, (tl + 2 * halo, 1), 0)
    t = pl.program_id(1) * tl - halo + e
    return (t >= 0) & (t < seq)


def _conv_kernel(xp_ref, xm_ref, xn_ref, mod_ref, ng_ref, win_ref, dw_ref, db_ref,
                 lng_ref, lnb_ref, wout_ref, o_ref, ybuf, zbuf, *, tl, seq):
    halo = CONV_HALO
    mod = mod_ref[0]
    xm = xm_ref[0]
    x_ext = jnp.concatenate([xp_ref[0], xm, xn_ref[0]], axis=0)
    h = _modulated(x_ext, ng_ref[...], mod).astype(BF16)
    valid = _row_valid(tl, seq, halo)

    base = halo - CONV_WIDTH // 2
    rs = CONV_ROW_STRIDE
    lb = CONV_LANE_BLOCK
    for c0 in range(0, D, lb):
        a = _dot(h, win_ref[:, c0:c0 + lb])
        b = _dot(h, win_ref[:, D + c0:D + c0 + lb])
        y = jnp.where(valid, _glu_half(a, b), 0.0)
        for cc in range(0, lb, 128):
            ybuf[(c0 + cc) // 128] = y[:, cc:cc + 128]
        for cc in range(0, lb, 128):
            slab = (c0 + cc) // 128
            taps = [jnp.broadcast_to(dw_ref[k:k + 1, c0 + cc:c0 + cc + 128], (8, 128)) for k in range(CONV_WIDTH)]
            for g0 in range(0, tl, CONV_GROUP_ROWS):
                rows0 = [t0 + q for t0 in range(g0, g0 + CONV_GROUP_ROWS, 8 * rs) for q in range(rs)]
                accs = [None] * len(rows0)
                for k in range(CONV_WIDTH):
                    for n, row in enumerate(rows0):
                        term = taps[k] * ybuf[slab, pl.ds(row + base + k, 8, stride=rs), :]
                        accs[n] = term if accs[n] is None else accs[n] + term
                for n, row in enumerate(rows0):
                    zbuf[slab, pl.ds(row, 8, stride=rs), :] = accs[n]
    sgate = _silu_half(_dot(h[halo:halo + tl], win_ref[:, 2 * D:3 * D]))
    z = jnp.concatenate([zbuf[sl] for sl in range(D // 128)], axis=1) + db_ref[...]
    u = _silu_half(_layernorm(z, lng_ref[...], lnb_ref[...])) * sgate
    o_ref[0] = xm + mod[2:3] * _dot(u.astype(BF16), wout_ref[...])


def _conv_layer(x, mod, ng, w_in, dw, db, ln_g, ln_b, w_out):
    bsz, seq, _ = x.shape
    tl = min(SEQ_TILE, seq)
    prev, main, nxt = _halo_specs(tl, seq, CONV_HALO)
    return pl.pallas_call(
        functools.partial(_conv_kernel, tl=tl, seq=seq),
        out_shape=jax.ShapeDtypeStruct(x.shape, F32),
        grid=(bsz, seq // tl),
        in_specs=[prev, main, nxt,
                  pl.BlockSpec((1, 3, D), lambda b, j: (b, 0, 0)),
                  _const_spec((1, D)),
                  _const_spec((D, 3 * D)),
                  _const_spec((CONV_WIDTH, D)),
                  _const_spec((1, D)), _const_spec((1, D)), _const_spec((1, D)),
                  _const_spec((D, D))],
        out_specs=main,
        scratch_shapes=[pltpu.VMEM((D // 128, tl + 2 * CONV_HALO, 128), F32),
                        pltpu.VMEM((D // 128, tl, 128), F32)],
        compiler_params=_params(2),
        name="conv_layer",
    )(x, x, x, mod, ng, w_in, dw, db, ln_g, ln_b, w_out)


def _pool_kernel(xp_ref, xm_ref, xn_ref, mod_ref, ng_ref, win_ref, wgrp_ref, scale_ref,
                 wout_ref, o_ref, vbuf, pbuf, *, tl, seq):
    halo = POOL_HALO
    mod = mod_ref[0]
    xm = xm_ref[0]
    x_ext = jnp.concatenate([xp_ref[0], xm, xn_ref[0]], axis=0)
    h = _modulated(x_ext, ng_ref[...], mod).astype(BF16)
    v = _dot(h, win_ref[:, 0:D])
    vbuf[...] = jnp.where(_row_valid(tl, seq, halo), v, 0.0)
    sgate = _silu_half(_dot(h[halo:halo + tl], win_ref[:, D:2 * D]))

    n = tl + 2 * halo
    wide0 = POOL_GROUP * POOL_LOG_FROM
    pbuf[0, 0:n - 2] = vbuf[0:n - 2, wide0:D] + vbuf[1:n - 1, wide0:D]
    pbuf[1, 0:n - 4] = pbuf[0, 0:n - 4] + pbuf[0, 2:n - 2]
    pbuf[0, 0:n - 8] = pbuf[1, 0:n - 8] + pbuf[1, 4:n - 4]

    t = pl.program_id(1) * tl + lax.broadcasted_iota(jnp.int32, (tl, 1), 0)
    ys = []
    for k, w in enumerate(POOL_WINDOWS):
        c0 = k * POOL_GROUP
        lo = halo - w // 2
        if w == 8:
            tot = pbuf[0, lo:lo + tl, c0 - wide0:c0 - wide0 + POOL_GROUP]
        elif w == 16:
            tot = (pbuf[0, lo:lo + tl, c0 - wide0:c0 - wide0 + POOL_GROUP]
                   + pbuf[0, lo + 8:lo + 8 + tl, c0 - wide0:c0 - wide0 + POOL_GROUP])
        else:
            tot = jnp.zeros((tl, POOL_GROUP), F32)
            for off in range(w):
                tot = tot + vbuf[lo + off:lo + off + tl, c0:c0 + POOL_GROUP]
        cnt = (jnp.minimum(t + (w - w // 2), seq) - jnp.maximum(t - w // 2, 0)).astype(F32)
        pooled = tot / cnt - vbuf[halo:halo + tl, c0:c0 + POOL_GROUP]
        ys.append(_dot(pooled.astype(BF16), wgrp_ref[k]))
    y = jnp.concatenate(ys, axis=1) * scale_ref[...]
    o_ref[0] = xm + mod[2:3] * _dot((y * sgate).astype(BF16), wout_ref[...])


def _pool_layer(x, mod, ng, w_in, w_grp, scale, w_out):
    bsz, seq, _ = x.shape
    tl = min(SEQ_TILE, seq)
    prev, main, nxt = _halo_specs(tl, seq, POOL_HALO)
    return pl.pallas_call(
        functools.partial(_pool_kernel, tl=tl, seq=seq),
        out_shape=jax.ShapeDtypeStruct(x.shape, F32),
        grid=(bsz, seq // tl),
        in_specs=[prev, main, nxt,
                  pl.BlockSpec((1, 3, D), lambda b, j: (b, 0, 0)),
                  _const_spec((1, D)),
                  _const_spec((D, 2 * D)),
                  _const_spec((len(POOL_WINDOWS), POOL_GROUP, POOL_GROUP)),
                  _const_spec((1, D)),
                  _const_spec((D, D))],
        out_specs=main,
        scratch_shapes=[pltpu.VMEM((tl + 2 * POOL_HALO, D), F32),
                        pltpu.VMEM((2, tl + 2 * POOL_HALO, D - POOL_GROUP * POOL_LOG_FROM), F32)],
        compiler_params=_params(2),
        name="pool_layer",
    )(x, x, x, mod, ng, w_in, w_grp, scale, w_out)


def _chunk_kernel(*refs, tl, has_pre):
    if has_pre:
        x_ref, po_ref, psg_ref, pmod_ref, pw_ref = refs[:5]
        refs = refs[5:]
        y0 = po_ref[0].astype(F32) * psg_ref[0].astype(F32)
        x = x_ref[0] + pmod_ref[0][2:3] * _dot(y0.astype(BF16), pw_ref[...])
    else:
        x = refs[0][0]
        refs = refs[1:]
    mod_ref, ng_ref, win_ref, lng_ref, lnb_ref, ws_ref, bs_ref, wout_ref, o_ref = refs
    mod = mod_ref[0]
    h = _modulated(x, ng_ref[...], mod).astype(BF16)
    u = _dot(h, win_ref[:, 0:D])
    v = _layernorm(_dot(h, win_ref[:, D:2 * D]), lng_ref[...], lnb_ref[...]).astype(BF16)
    sgate = _silu_half(_dot(h, win_ref[:, 2 * D:3 * D]))
    nch = tl // CHUNK
    gc = D // CHUNK_GROUPS
    cols = []
    for g in range(CHUNK_GROUPS):
        vg = jnp.concatenate([v[n * CHUNK:(n + 1) * CHUNK, g * gc:(g + 1) * gc] for n in range(nch)], axis=1)
        sg = _dot(ws_ref[g], vg)
        cols.append(jnp.concatenate([sg[:, n * gc:(n + 1) * gc] for n in range(nch)], axis=0))
    s = jnp.concatenate(cols, axis=1) + jnp.concatenate([bs_ref[...]] * nch, axis=0)
    y = u * s * sgate
    o_ref[0] = x + mod[2:3] * _dot(y.astype(BF16), wout_ref[...])


def _chunk_layer(x, mod, ng, w_in, ln_g, ln_b, w_s, b_s, w_out, pre=None):
    bsz, seq, _ = x.shape
    tl = min(SEQ_TILE, seq)
    main = pl.BlockSpec((1, tl, D), lambda b, j: (b, j, 0))
    modspec = pl.BlockSpec((1, 3, D), lambda b, j: (b, 0, 0))
    pre_specs, pre_args = [], []
    if pre is not None:
        pre_specs = [main, main, modspec, _const_spec((D, D))]
        pre_args = list(pre)
    return pl.pallas_call(
        functools.partial(_chunk_kernel, tl=tl, has_pre=pre is not None),
        out_shape=jax.ShapeDtypeStruct(x.shape, F32),
        grid=(bsz, seq // tl),
        in_specs=[main] + pre_specs +
                 [modspec,
                  _const_spec((1, D)),
                  _const_spec((D, 3 * D)),
                  _const_spec((1, D)), _const_spec((1, D)),
                  _const_spec((CHUNK_GROUPS, CHUNK, CHUNK)),
                  _const_spec((CHUNK, D)),
                  _const_spec((D, D))],
        out_specs=main,
        compiler_params=_params(2),
        name="chunk_layer",
    )(x, *pre_args, mod, ng, w_in, ln_g, ln_b, w_s, b_s, w_out)


def _rope_tables(seq):
    rows = seq // GRID_W
    row_id = np.repeat(np.arange(rows), GRID_W).astype(np.float32)
    col_id = np.tile(np.arange(GRID_W), rows).astype(np.float32)
    axis_dim = ROPE // 2
    freqs = (np.float32(ROPE_THETA) ** (-np.arange(0, axis_dim, 2, dtype=np.float32) / axis_dim)).astype(np.float32)
    ar = row_id[:, None] * freqs
    ac = col_id[:, None] * freqs
    q = axis_dim // 2
    zeros = np.zeros((seq, q), np.float32)
    pad = np.zeros((seq, 128 - ROPE), np.float32)
    cos = np.concatenate([np.cos(ar), np.cos(ar), np.cos(ac), np.cos(ac), pad], axis=1)
    sin_up = np.concatenate([-np.sin(ar), zeros, -np.sin(ac), zeros, pad], axis=1)
    sin_dn = np.concatenate([zeros, np.sin(ar), zeros, np.sin(ac), pad], axis=1)
    return (jnp.asarray(cos, F32), jnp.asarray(sin_up, F32), jnp.asarray(sin_dn, F32))


def _rope(x, cos, sin_up, sin_dn):
    quarter = ROPE // 4
    up = pltpu.roll(x, 128 - quarter, axis=1)
    dn = pltpu.roll(x, quarter, axis=1)
    return x * cos + up * sin_up + dn * sin_dn


def _mla_keys(ckv, kr, kvn_ref, wukv_ref, kng_ref, krg_ref, tabs, k_ref, v_ref):
    kv = _dot(_rms(ckv, kvn_ref[...]).astype(BF16), wukv_ref[...])
    krn = _rms(kr, krg_ref[...], n=ROPE)
    if tabs is not None:
        krn = _rope(krn, *tabs)
    krn = krn.astype(BF16)
    kng = kng_ref[...]
    for hd in range(HEADS):
        c0 = hd * HEAD_QK
        k_ref[0, :, c0:c0 + NOPE] = _rms(kv[:, c0:c0 + NOPE], kng).astype(BF16)
        k_ref[0, :, c0 + NOPE:c0 + HEAD_QK] = krn
        v_ref[0, :, hd * HEAD_V:(hd + 1) * HEAD_V] = kv[:, c0 + NOPE:c0 + HEAD_QK].astype(BF16)


def _mla_proj_kernel(x_ref, mod_ref, ng_ref, win_ref, qn_ref, kvn_ref, wuq_ref, wukv_ref,
                     qng_ref, qrg_ref, kng_ref, krg_ref, cos_ref, sup_ref, sdn_ref,
                     q_ref, k_ref, v_ref, sg_ref):
    h = _modulated(x_ref[0], ng_ref[...], mod_ref[0]).astype(BF16)
    p = _dot(h, win_ref[...])
    tabs = (cos_ref[...], sup_ref[...], sdn_ref[...])
    _mla_keys(p[:, 0:KV_RANK], p[:, KV_RANK:KV_RANK + 128], kvn_ref, wukv_ref, kng_ref, krg_ref,
              tabs, k_ref, v_ref)
    c1 = KV_RANK + 128
    q = _dot(_rms(p[:, c1:c1 + Q_RANK], qn_ref[...]).astype(BF16), wuq_ref[...])
    qng = qng_ref[...]
    qrg = qrg_ref[...]
    qs = SCORE_SCALE * LOG2E
    for hd in range(HEADS):
        c0 = hd * HEAD_QK
        q_ref[0, :, c0:c0 + NOPE] = (_rms(q[:, c0:c0 + NOPE], qng) * qs).astype(BF16)
        qr = _rope(_rms(q[:, c0 + NOPE:c0 + HEAD_QK], qrg, n=ROPE), *tabs)
        q_ref[0, :, c0 + NOPE:c0 + HEAD_QK] = (qr * qs).astype(BF16)
    sg_ref[0] = _silu_half(p[:, c1 + Q_RANK:c1 + Q_RANK + D]).astype(BF16)


def _mla_ctx_kernel(x_ref, mod_ref, ng_ref, win_ref, kvn_ref, wukv_ref, kng_ref, krg_ref,
                    k_ref, v_ref):
    h = _modulated(x_ref[0], ng_ref[...], mod_ref[0]).astype(BF16)
    p = _dot(h, win_ref[...])
    _mla_keys(p[:, 0:KV_RANK], p[:, KV_RANK:KV_RANK + 128], kvn_ref, wukv_ref, kng_ref, krg_ref,
              None, k_ref, v_ref)


def _attn_kernel(q_ref, kl_ref, vl_ref, kc_ref, vc_ref, o_ref, *, sub):
    for r0 in range(0, q_ref.shape[1], sub):
        q = q_ref[0, r0:r0 + sub, :]
        s1 = _dot_nt(q, kl_ref[0])
        s2 = _dot_nt(q, kc_ref[0])
        m = jnp.maximum(jnp.max(s1, axis=-1, keepdims=True), jnp.max(s2, axis=-1, keepdims=True))
        p1 = jnp.exp2(s1 - m)
        p2 = jnp.exp2(s2 - m)
        l = jnp.sum(p1, axis=-1, keepdims=True) + jnp.sum(p2, axis=-1, keepdims=True)
        o = _dot(p1.astype(BF16), vl_ref[0]) + _dot(p2.astype(BF16), vc_ref[0])
        o_ref[0, r0:r0 + sub, :] = (o / l).astype(BF16)


def _attention(q, k, v, kc, vc):
    bsz, seq, _ = q.shape
    cseq = kc.shape[1]
    tq = min(ATTN_Q_TILE, seq)
    return pl.pallas_call(
        functools.partial(_attn_kernel, sub=min(ATTN_Q_SUB, tq)),
        out_shape=jax.ShapeDtypeStruct((bsz, seq, HEADS * HEAD_V), BF16),
        grid=(bsz, HEADS, seq // tq),
        in_specs=[pl.BlockSpec((1, tq, HEAD_QK), lambda b, h, i: (b, i, h)),
                  pl.BlockSpec((1, seq, HEAD_QK), lambda b, h, i: (b, 0, h)),
                  pl.BlockSpec((1, seq, HEAD_V), lambda b, h, i: (b, 0, h)),
                  pl.BlockSpec((1, cseq, HEAD_QK), lambda b, h, i: (b, 0, h)),
                  pl.BlockSpec((1, cseq, HEAD_V), lambda b, h, i: (b, 0, h))],
        out_specs=pl.BlockSpec((1, tq, HEAD_V), lambda b, h, i: (b, i, h)),
        compiler_params=_params(3),
        name="mla_attention",
    )(q, k, v, kc, vc)


def _outproj_kernel(x_ref, mod_ref, o_ref, sg_ref, wout_ref, out_ref):
    y = o_ref[0].astype(F32) * sg_ref[0].astype(F32)
    out_ref[0] = x_ref[0] + mod_ref[0][2:3] * _dot(y.astype(BF16), wout_ref[...])


def _mla_layer(x, cx, mod, mod_c, ng, w_in, q_norm, kv_norm, w_uq, w_ukv, nope_g, rope_g, w_out, defer_out):
    bsz, seq, _ = x.shape
    cseq = cx.shape[1]
    kvc = KV_RANK + ROPE
    w_kv_in = jnp.pad(w_in[:, :kvc], ((0, 0), (0, 128 - ROPE)))
    gate0 = kvc + Q_RANK
    w_in_p = jnp.concatenate([w_kv_in, w_in[:, kvc:gate0], 0.5 * w_in[:, gate0:]], axis=1).astype(BF16)
    w_kv_in = w_kv_in.astype(BF16)
    w_uq_p = jnp.pad(w_uq.reshape(Q_RANK, HEADS, NOPE + ROPE),
                     ((0, 0), (0, 0), (0, HEAD_QK - NOPE - ROPE))).reshape(Q_RANK, HEADS * HEAD_QK).astype(BF16)
    w_ukv_b = w_ukv.astype(BF16)
    pad_g = lambda g: jnp.pad(g, (0, 128 - ROPE)).reshape(1, 128)
    qng, kng = nope_g[0].reshape(1, NOPE), nope_g[1].reshape(1, NOPE)
    qrg, krg = pad_g(rope_g[0]), pad_g(rope_g[1])
    cos, sup, sdn = _rope_tables(seq)

    tr = MLA_ROW_TILE
    win_cols = w_in_p.shape[1]
    row = lambda w: pl.BlockSpec((1, tr, w), lambda b, j: (b, j, 0))
    modspec = pl.BlockSpec((1, 3, D), lambda b, j: (b, 0, 0))
    tab = pl.BlockSpec((tr, 128), lambda b, j: (j, 0))
    q, k, v, sg = pl.pallas_call(
        _mla_proj_kernel,
        out_shape=(jax.ShapeDtypeStruct((bsz, seq, HEADS * HEAD_QK), BF16),
                   jax.ShapeDtypeStruct((bsz, seq, HEADS * HEAD_QK), BF16),
                   jax.ShapeDtypeStruct((bsz, seq, HEADS * HEAD_V), BF16),
                   jax.ShapeDtypeStruct((bsz, seq, D), BF16)),
        grid=(bsz, seq // tr),
        in_specs=[row(D), modspec, _const_spec((1, D)),
                  _const_spec((D, win_cols)),
                  _const_spec((1, Q_RANK)), _const_spec((1, KV_RANK)),
                  _const_spec((Q_RANK, HEADS * HEAD_QK)), _const_spec((KV_RANK, HEADS * HEAD_QK)),
                  _const_spec((1, NOPE)), _const_spec((1, 128)), _const_spec((1, NOPE)), _const_spec((1, 128)),
                  tab, tab, tab],
        out_specs=(row(HEADS * HEAD_QK), row(HEADS * HEAD_QK), row(HEADS * HEAD_V), row(D)),
        compiler_params=_params(2),
        name="mla_proj",
    )(x, mod, ng, w_in_p, q_norm.reshape(1, -1), kv_norm.reshape(1, -1), w_uq_p, w_ukv_b,
      qng, qrg, kng, krg, cos, sup, sdn)

    trc = min(tr, cseq)
    rowc = lambda w: pl.BlockSpec((1, trc, w), lambda b, j: (b, j, 0))
    kc, vc = pl.pallas_call(
        _mla_ctx_kernel,
        out_shape=(jax.ShapeDtypeStruct((bsz, cseq, HEADS * HEAD_QK), BF16),
                   jax.ShapeDtypeStruct((bsz, cseq, HEADS * HEAD_V), BF16)),
        grid=(bsz, cseq // trc),
        in_specs=[rowc(D), modspec, _const_spec((1, D)),
                  _const_spec((D, KV_RANK + 128)),
                  _const_spec((1, KV_RANK)),
                  _const_spec((KV_RANK, HEADS * HEAD_QK)),
                  _const_spec((1, NOPE)), _const_spec((1, 128))],
        out_specs=(rowc(HEADS * HEAD_QK), rowc(HEADS * HEAD_V)),
        compiler_params=_params(2),
        name="mla_ctx_proj",
    )(cx, mod_c, ng, w_kv_in, kv_norm.reshape(1, -1), w_ukv_b, kng, krg)

    o = _attention(q, k, v, kc, vc)
    if defer_out:
        return x, (o, sg, mod, w_out.astype(BF16))

    tl = min(SEQ_TILE, seq)
    main = lambda dt: pl.BlockSpec((1, tl, D), lambda b, j: (b, j, 0))
    return pl.pallas_call(
        _outproj_kernel,
        out_shape=jax.ShapeDtypeStruct(x.shape, F32),
        grid=(bsz, seq // tl),
        in_specs=[main(F32), modspec, main(BF16), main(BF16), _const_spec((D, D))],
        out_specs=main(F32),
        compiler_params=_params(2),
        name="mla_outproj",
    )(x, mod, o, sg, w_out.astype(BF16)), None


def kernel(x, c, ctx, c_ctx, norm_g, w_mod, b_mod, cv_w_in, cv_dw, cv_db, cv_ln_g, cv_ln_b, cv_w_out, pl_w_in, pl_w_grp, pl_scale, pl_w_out, ml_w_in, ml_q_norm, ml_kv_norm, ml_w_uq, ml_w_ukv, ml_nope_norm, ml_rope_norm, ml_w_out, ch_w_in, ch_ln_g, ch_ln_b, ch_w_s, ch_b_s, ch_w_out):
    bsz = x.shape[0]
    rows = -(-(bsz + 1) // 8) * 8
    cvec = jnp.concatenate([c, c_ctx[None, :], jnp.zeros((rows - bsz - 1, D), F32)], axis=0)
    mods = _modulation(cvec, w_mod, b_mod)
    row2 = lambda a: a.reshape(1, -1)
    cx = ctx
    pending = None
    for i in range(DEPTH):
        kind, j = i % 4, i // 4
        ctx_out = any(k % 4 == 2 for k in range(i + 1, DEPTH))
        ctx_in = ctx_out or kind == 2
        mod = mods[i, :bsz].reshape(bsz, 3, D)
        mod_c = jnp.broadcast_to(mods[i, bsz].reshape(1, 3, D), (bsz, 3, D))
        ng = row2(norm_g[i])
        if kind == 0:
            args = ((0.5 * cv_w_in[j]).astype(BF16), cv_dw[j], row2(cv_db[j]), row2(0.5 * cv_ln_g[j]),
                    row2(0.5 * cv_ln_b[j]), cv_w_out[j].astype(BF16))
            layer = _conv_layer
        elif kind == 1:
            w_in = jnp.concatenate([pl_w_in[j][:, :D], 0.5 * pl_w_in[j][:, D:]], axis=1)
            args = (w_in.astype(BF16), pl_w_grp[j].astype(BF16), row2(pl_scale[j]), pl_w_out[j].astype(BF16))
            layer = _pool_layer
        elif kind == 3:
            bias = jnp.repeat(ch_b_s[j], D // CHUNK_GROUPS, axis=1)
            w_in = jnp.concatenate([ch_w_in[j][:, :2 * D], 0.5 * ch_w_in[j][:, 2 * D:]], axis=1)
            args = (w_in.astype(BF16), row2(ch_ln_g[j]), row2(ch_ln_b[j]), ch_w_s[j].astype(BF16), bias,
                    ch_w_out[j].astype(BF16))
            layer = _chunk_layer
        if kind == 2:
            assert not ctx_out, "context queries are only needed when a later layer reads the context"
            defer = i + 1 < DEPTH and (i + 1) % 4 == 3
            x, pending = _mla_layer(x, cx, mod, mod_c, ng, ml_w_in[j], ml_q_norm[j], ml_kv_norm[j], ml_w_uq[j],
                                    ml_w_ukv[j], ml_nope_norm[j], ml_rope_norm[j], ml_w_out[j], defer)
        else:
            if ctx_out:
                cx = layer(cx, mod_c, ng, *args)
            if kind == 3:
                x = layer(x, mod, ng, *args, pre=pending)
                pending = None
            else:
                x = layer(x, mod, ng, *args)
        del ctx_in
    return x
```

```python
import functools

import numpy as np
import jax
import jax.numpy as jnp
from jax import lax
from jax.experimental import pallas as pl
from jax.experimental.pallas import tpu as pltpu

D = 1024
DEPTH = 4
EPS = 1e-6
GRID_W = 64
CONV_WIDTH = 31
CONV_HALO = 16
CONV_ROW_STRIDE = 4
CONV_LANE_BLOCK = 256
CONV_GROUP_ROWS = 64
POOL_WINDOWS = (2, 4, 8, 16)
POOL_GROUP = D // len(POOL_WINDOWS)
POOL_HALO = 8
POOL_LOG_FROM = 2
HEADS = 8
NOPE = 128
ROPE = 64
HEAD_V = 128
HEAD_QK = 256
Q_RANK = 384
KV_RANK = 256
ROPE_THETA = 10000.0
SCORE_SCALE = (NOPE + ROPE) ** -0.5
LOG2E = 1.4426950408889634
CHUNK = 128
CHUNK_GROUPS = 8
SEQ_TILE = 512
MLA_ROW_TILE = 256
ATTN_Q_TILE = 2048
ATTN_Q_SUB = 512
ATTN_MAX_SHIFT = 50.0
VMEM_LIMIT = 56 * 1024 * 1024

F32 = jnp.float32
BF16 = jnp.bfloat16


def _silu_half(hx):
    return hx + hx * jnp.tanh(hx)


def _glu_half(ha, hb):
    return ha + ha * jnp.tanh(hb)


def _silu(x):
    return _silu_half(0.5 * x)


def _dot(a, b):
    return jnp.dot(a, b, preferred_element_type=F32)


def _dot_nt(a, b):
    return lax.dot_general(a, b, (((1,), (1,)), ((), ())), preferred_element_type=F32)


def _rms(x, g, n=None):
    n = x.shape[-1] if n is None else n
    ms = jnp.sum(x * x, axis=-1, keepdims=True) * (1.0 / n)
    return (x * lax.rsqrt(ms + EPS)) * g


def _layernorm(x, g, b):
    mu = jnp.mean(x, axis=-1, keepdims=True)
    xc = x - mu
    var = jnp.mean(xc * xc, axis=-1, keepdims=True)
    return (xc * lax.rsqrt(var + EPS)) * g + b


def _modulated(x, ng, mod):
    return _rms(x, ng) * (1.0 + mod[1:2]) + mod[0:1]


def _const_spec(shape):
    nd = len(shape)
    return pl.BlockSpec(shape, lambda *_: (0,) * nd)


def _params(n_axes):
    return pltpu.CompilerParams(dimension_semantics=("arbitrary",) * n_axes,
                                vmem_limit_bytes=VMEM_LIMIT)


def _mod_kernel(cv_ref, w_ref, b_ref, o_ref):
    s = _silu(cv_ref[...])
    o_ref[0] = _dot(s.astype(BF16), w_ref[0].astype(BF16)) + b_ref[0]


def _modulation(cvec, w_mod, b_mod):
    rows = cvec.shape[0]
    nt = 3
    return pl.pallas_call(
        _mod_kernel,
        out_shape=jax.ShapeDtypeStruct((DEPTH, rows, 3 * D), F32),
        grid=(DEPTH, nt),
        in_specs=[_const_spec((rows, D)),
                  pl.BlockSpec((1, D, D), lambda i, n: (i, 0, n)),
                  pl.BlockSpec((1, 1, D), lambda i, n: (i, 0, n))],
        out_specs=pl.BlockSpec((1, rows, D), lambda i, n: (i, 0, n)),
        compiler_params=_params(2),
        name="modulation",
    )(cvec, w_mod, b_mod.reshape(DEPTH, 1, 3 * D))


def _halo_specs(tl, seq, halo):
    per = tl // halo
    last = seq // halo - 1
    prev = pl.BlockSpec((1, halo, D), lambda b, j: (b, jnp.maximum(j * per - 1, 0), 0))
    main = pl.BlockSpec((1, tl, D), lambda b, j: (b, j, 0))
    nxt = pl.BlockSpec((1, halo, D), lambda b, j: (b, jnp.minimum((j + 1) * per, last), 0))
    return prev, main, nxt


def _row_valid(tl, seq, halo):
    e = lax.broadcasted_iota(jnp.int32, (tl + 2 * halo, 1), 0)
    t = pl.program_id(1) * tl - halo + e
    return (t >= 0) & (t < seq)


def _conv_kernel(xp_ref, xm_ref, xn_ref, mod_ref, ng_ref, win_ref, dw_ref, db_ref,
                 lng_ref, lnb_ref, wout_ref, o_ref, ybuf, zbuf, *, tl, seq):
    halo = CONV_HALO
    mod = mod_ref[0]
    xm = xm_ref[0]
    x_ext = jnp.concatenate([xp_ref[0], xm, xn_ref[0]], axis=0)
    h = _modulated(x_ext, ng_ref[...], mod).astype(BF16)
    valid = _row_valid(tl, seq, halo)

    base = halo - CONV_WIDTH // 2
    rs = CONV_ROW_STRIDE
    lb = CONV_LANE_BLOCK
    for c0 in range(0, D, lb):
        a = _dot(h, win_ref[:, c0:c0 + lb])
        b = _dot(h, win_ref[:, D + c0:D + c0 + lb])
        y = jnp.where(valid, _glu_half(a, b), 0.0)
        for cc in range(0, lb, 128):
            ybuf[(c0 + cc) // 128] = y[:, cc:cc + 128]
        for cc in range(0, lb, 128):
            slab = (c0 + cc) // 128
            taps = [jnp.broadcast_to(dw_ref[k:k + 1, c0 + cc:c0 + cc + 128], (8, 128)) for k in range(CONV_WIDTH)]
            for g0 in range(0, tl, CONV_GROUP_ROWS):
                rows0 = [t0 + q for t0 in range(g0, g0 + CONV_GROUP_ROWS, 8 * rs) for q in range(rs)]
                accs = [None] * len(rows0)
                for k in range(CONV_WIDTH):
                    for n, row in enumerate(rows0):
                        term = taps[k] * ybuf[slab, pl.ds(row + base + k, 8, stride=rs), :]
                        accs[n] = term if accs[n] is None else accs[n] + term
                for n, row in enumerate(rows0):
                    zbuf[slab, pl.ds(row, 8, stride=rs), :] = accs[n]
    sgate = _silu_half(_dot(h[halo:halo + tl], win_ref[:, 2 * D:3 * D]))
    z = jnp.concatenate([zbuf[sl] for sl in range(D // 128)], axis=1) + db_ref[...]
    u = _silu_half(_layernorm(z, lng_ref[...], lnb_ref[...])) * sgate
    o_ref[0] = xm + mod[2:3] * _dot(u.astype(BF16), wout_ref[...])


def _conv_layer(x, mod, ng, w_in, dw, db, ln_g, ln_b, w_out):
    bsz, seq, _ = x.shape
    tl = min(SEQ_TILE, seq)
    prev, main, nxt = _halo_specs(tl, seq, CONV_HALO)
    return pl.pallas_call(
        functools.partial(_conv_kernel, tl=tl, seq=seq),
        out_shape=jax.ShapeDtypeStruct(x.shape, F32),
        grid=(bsz, seq // tl),
        in_specs=[prev, main, nxt,
                  pl.BlockSpec((1, 3, D), lambda b, j: (b, 0, 0)),
                  _const_spec((1, D)),
                  _const_spec((D, 3 * D)),
                  _const_spec((CONV_WIDTH, D)),
                  _const_spec((1, D)), _const_spec((1, D)), _const_spec((1, D)),
                  _const_spec((D, D))],
        out_specs=main,
        scratch_shapes=[pltpu.VMEM((D // 128, tl + 2 * CONV_HALO, 128), F32),
                        pltpu.VMEM((D // 128, tl, 128), F32)],
        compiler_params=_params(2),
        name="conv_layer",
    )(x, x, x, mod, ng, w_in, dw, db, ln_g, ln_b, w_out)


def _pool_kernel(xp_ref, xm_ref, xn_ref, mod_ref, ng_ref, win_ref, wgrp_ref, scale_ref,
                 wout_ref, o_ref, vbuf, pbuf, *, tl, seq):
    halo = POOL_HALO
    mod = mod_ref[0]
    xm = xm_ref[0]
    x_ext = jnp.concatenate([xp_ref[0], xm, xn_ref[0]], axis=0)
    h = _modulated(x_ext, ng_ref[...], mod).astype(BF16)
    v = _dot(h, win_ref[:, 0:D])
    vbuf[...] = jnp.where(_row_valid(tl, seq, halo), v, 0.0)
    sgate = _silu_half(_dot(h[halo:halo + tl], win_ref[:, D:2 * D]))

    n = tl + 2 * halo
    wide0 = POOL_GROUP * POOL_LOG_FROM
    pbuf[0, 0:n - 2] = vbuf[0:n - 2, wide0:D] + vbuf[1:n - 1, wide0:D]
    pbuf[1, 0:n - 4] = pbuf[0, 0:n - 4] + pbuf[0, 2:n - 2]
    pbuf[0, 0:n - 8] = pbuf[1, 0:n - 8] + pbuf[1, 4:n - 4]

    t = pl.program_id(1) * tl + lax.broadcasted_iota(jnp.int32, (tl, 1), 0)
    ys = []
    for k, w in enumerate(POOL_WINDOWS):
        c0 = k * POOL_GROUP
        lo = halo - w // 2
        if w == 8:
            tot = pbuf[0, lo:lo + tl, c0 - wide0:c0 - wide0 + POOL_GROUP]
        elif w == 16:
            tot = (pbuf[0, lo:lo + tl, c0 - wide0:c0 - wide0 + POOL_GROUP]
                   + pbuf[0, lo + 8:lo + 8 + tl, c0 - wide0:c0 - wide0 + POOL_GROUP])
        else:
            tot = jnp.zeros((tl, POOL_GROUP), F32)
            for off in range(w):
                tot = tot + vbuf[lo + off:lo + off + tl, c0:c0 + POOL_GROUP]
        cnt = (jnp.minimum(t + (w - w // 2), seq) - jnp.maximum(t - w // 2, 0)).astype(F32)
        pooled = tot / cnt - vbuf[halo:halo + tl, c0:c0 + POOL_GROUP]
        ys.append(_dot(pooled.astype(BF16), wgrp_ref[k]))
    y = jnp.concatenate(ys, axis=1) * scale_ref[...]
    o_ref[0] = xm + mod[2:3] * _dot((y * sgate).astype(BF16), wout_ref[...])


def _pool_layer(x, mod, ng, w_in, w_grp, scale, w_out):
    bsz, seq, _ = x.shape
    tl = min(SEQ_TILE, seq)
    prev, main, nxt = _halo_specs(tl, seq, POOL_HALO)
    return pl.pallas_call(
        functools.partial(_pool_kernel, tl=tl, seq=seq),
        out_shape=jax.ShapeDtypeStruct(x.shape, F32),
        grid=(bsz, seq // tl),
        in_specs=[prev, main, nxt,
                  pl.BlockSpec((1, 3, D), lambda b, j: (b, 0, 0)),
                  _const_spec((1, D)),
                  _const_spec((D, 2 * D)),
                  _const_spec((len(POOL_WINDOWS), POOL_GROUP, POOL_GROUP)),
                  _const_spec((1, D)),
                  _const_spec((D, D))],
        out_specs=main,
        scratch_shapes=[pltpu.VMEM((tl + 2 * POOL_HALO, D), F32),
                        pltpu.VMEM((2, tl + 2 * POOL_HALO, D - POOL_GROUP * POOL_LOG_FROM), F32)],
        compiler_params=_params(2),
        name="pool_layer",
    )(x, x, x, mod, ng, w_in, w_grp, scale, w_out)


def _chunk_kernel(*refs, tl, has_pre):
    if has_pre:
        x_ref, po_ref, psg_ref, pmod_ref, pw_ref = refs[:5]
        refs = refs[5:]
        o = jnp.concatenate([po_ref[0, hd].astype(F32).T for hd in range(HEADS)], axis=1)
        y0 = o * psg_ref[0].astype(F32)
        x = x_ref[0] + pmod_ref[0][2:3] * _dot(y0.astype(BF16), pw_ref[...])
    else:
        x = refs[0][0]
        refs = refs[1:]
    mod_ref, ng_ref, win_ref, lng_ref, lnb_ref, ws_ref, bs_ref, wout_ref, o_ref = refs
    mod = mod_ref[0]
    h = _modulated(x, ng_ref[...], mod).astype(BF16)
    u = _dot(h, win_ref[:, 0:D])
    v = _layernorm(_dot(h, win_ref[:, D:2 * D]), lng_ref[...], lnb_ref[...]).astype(BF16)
    sgate = _silu_half(_dot(h, win_ref[:, 2 * D:3 * D]))
    nch = tl // CHUNK
    gc = D // CHUNK_GROUPS
    cols = []
    for g in range(CHUNK_GROUPS):
        vg = jnp.concatenate([v[n * CHUNK:(n + 1) * CHUNK, g * gc:(g + 1) * gc] for n in range(nch)], axis=1)
        sg = _dot(ws_ref[g], vg)
        cols.append(jnp.concatenate([sg[:, n * gc:(n + 1) * gc] for n in range(nch)], axis=0))
    s = jnp.concatenate(cols, axis=1) + jnp.concatenate([bs_ref[...]] * nch, axis=0)
    y = u * s * sgate
    o_ref[0] = x + mod[2:3] * _dot(y.astype(BF16), wout_ref[...])


def _chunk_layer(x, mod, ng, w_in, ln_g, ln_b, w_s, b_s, w_out, pre=None):
    bsz, seq, _ = x.shape
    tl = min(SEQ_TILE, seq)
    main = pl.BlockSpec((1, tl, D), lambda b, j: (b, j, 0))
    modspec = pl.BlockSpec((1, 3, D), lambda b, j: (b, 0, 0))
    pre_specs, pre_args = [], []
    if pre is not None:
        pre_specs = [pl.BlockSpec((1, HEADS, HEAD_V, tl), lambda b, j: (b, 0, 0, j)), main, modspec,
                     _const_spec((D, D))]
        pre_args = list(pre)
    return pl.pallas_call(
        functools.partial(_chunk_kernel, tl=tl, has_pre=pre is not None),
        out_shape=jax.ShapeDtypeStruct(x.shape, F32),
        grid=(bsz, seq // tl),
        in_specs=[main] + pre_specs +
                 [modspec,
                  _const_spec((1, D)),
                  _const_spec((D, 3 * D)),
                  _const_spec((1, D)), _const_spec((1, D)),
                  _const_spec((CHUNK_GROUPS, CHUNK, CHUNK)),
                  _const_spec((CHUNK, D)),
                  _const_spec((D, D))],
        out_specs=main,
        compiler_params=_params(2),
        name="chunk_layer",
    )(x, *pre_args, mod, ng, w_in, ln_g, ln_b, w_s, b_s, w_out)


def _rope_tables(seq):
    rows = seq // GRID_W
    row_id = np.repeat(np.arange(rows), GRID_W).astype(np.float32)
    col_id = np.tile(np.arange(GRID_W), rows).astype(np.float32)
    axis_dim = ROPE // 2
    freqs = (np.float32(ROPE_THETA) ** (-np.arange(0, axis_dim, 2, dtype=np.float32) / axis_dim)).astype(np.float32)
    ar = row_id[:, None] * freqs
    ac = col_id[:, None] * freqs
    q = axis_dim // 2
    zeros = np.zeros((seq, q), np.float32)
    pad = np.zeros((seq, 128 - ROPE), np.float32)
    cos = np.concatenate([np.cos(ar), np.cos(ar), np.cos(ac), np.cos(ac), pad], axis=1)
    sin_up = np.concatenate([-np.sin(ar), zeros, -np.sin(ac), zeros, pad], axis=1)
    sin_dn = np.concatenate([zeros, np.sin(ar), zeros, np.sin(ac), pad], axis=1)
    return (jnp.asarray(cos, F32), jnp.asarray(sin_up, F32), jnp.asarray(sin_dn, F32))


def _rope(x, cos, sin_up, sin_dn):
    quarter = ROPE // 4
    up = pltpu.roll(x, 128 - quarter, axis=1)
    dn = pltpu.roll(x, quarter, axis=1)
    return x * cos + up * sin_up + dn * sin_dn


def _mla_keys(ckv, kr, kvn_ref, wukv_ref, kng_ref, krg_ref, tabs, k_ref, v_ref):
    kv = _dot(_rms(ckv, kvn_ref[...]).astype(BF16), wukv_ref[...])
    krn = _rms(kr, krg_ref[...], n=ROPE)
    if tabs is not None:
        krn = _rope(krn, *tabs)
    krn = (krn + (lax.broadcasted_iota(jnp.int32, (1, 128), 1) == ROPE).astype(F32)).astype(BF16)
    kng = kng_ref[...]
    for hd in range(HEADS):
        c0 = hd * HEAD_QK
        k_ref[0, :, c0:c0 + NOPE] = _rms(kv[:, c0:c0 + NOPE], kng).astype(BF16)
        k_ref[0, :, c0 + NOPE:c0 + HEAD_QK] = krn
        v_ref[0, hd] = kv[:, c0 + NOPE:c0 + HEAD_QK].T.astype(BF16)


def _mla_proj_kernel(x_ref, mod_ref, ng_ref, win_ref, qn_ref, kvn_ref, wuq_ref, wukv_ref,
                     qng_ref, qrg_ref, kng_ref, krg_ref, qshift_ref, cos_ref, sup_ref, sdn_ref,
                     q_ref, k_ref, v_ref, sg_ref):
    h = _modulated(x_ref[0], ng_ref[...], mod_ref[0]).astype(BF16)
    p = _dot(h, win_ref[...])
    tabs = (cos_ref[...], sup_ref[...], sdn_ref[...])
    _mla_keys(p[:, 0:KV_RANK], p[:, KV_RANK:KV_RANK + 128], kvn_ref, wukv_ref, kng_ref, krg_ref,
              tabs, k_ref, v_ref)
    c1 = KV_RANK + 128
    q = _dot(_rms(p[:, c1:c1 + Q_RANK], qn_ref[...]).astype(BF16), wuq_ref[...])
    qng = qng_ref[...]
    qrg = qrg_ref[...]
    qs = SCORE_SCALE * LOG2E
    for hd in range(HEADS):
        c0 = hd * HEAD_QK
        q_ref[0, :, c0:c0 + NOPE] = (_rms(q[:, c0:c0 + NOPE], qng) * qs).astype(BF16)
        qr = _rope(_rms(q[:, c0 + NOPE:c0 + HEAD_QK], qrg, n=ROPE), *tabs)
        q_ref[0, :, c0 + NOPE:c0 + HEAD_QK] = (qr * qs + qshift_ref[...]).astype(BF16)
    sg_ref[0] = _silu_half(p[:, c1 + Q_RANK:c1 + Q_RANK + D]).astype(BF16)


def _mla_ctx_kernel(x_ref, mod_ref, ng_ref, win_ref, kvn_ref, wukv_ref, kng_ref, krg_ref,
                    k_ref, v_ref):
    h = _modulated(x_ref[0], ng_ref[...], mod_ref[0]).astype(BF16)
    p = _dot(h, win_ref[...])
    _mla_keys(p[:, 0:KV_RANK], p[:, KV_RANK:KV_RANK + 128], kvn_ref, wukv_ref, kng_ref, krg_ref,
              None, k_ref, v_ref)


def _attn_kernel(q_ref, kl_ref, vlt_ref, kc_ref, vct_ref, o_ref, *, sub, bounded):
    for r0 in range(0, q_ref.shape[1], sub):
        q = q_ref[0, r0:r0 + sub, :]
        s1 = _dot_nt(kl_ref[0], q)
        s2 = _dot_nt(kc_ref[0], q)
        if not bounded:
            m = jnp.maximum(jnp.max(s1, axis=0, keepdims=True), jnp.max(s2, axis=0, keepdims=True))
            s1 = s1 - m
            s2 = s2 - m
        p1 = jnp.exp2(s1)
        p2 = jnp.exp2(s2)
        l = jnp.sum(p1, axis=0, keepdims=True) + jnp.sum(p2, axis=0, keepdims=True)
        o = _dot(vlt_ref[0, 0], p1.astype(BF16)) + _dot(vct_ref[0, 0], p2.astype(BF16))
        o_ref[0, 0, :, r0:r0 + sub] = (o / l).astype(BF16)


def _attention(q, k, vt, kc, vct, bounded):
    bsz, seq, _ = q.shape
    cseq = kc.shape[1]
    tq = min(ATTN_Q_TILE, seq)
    sub = tq if bounded else min(ATTN_Q_SUB, tq)
    return pl.pallas_call(
        functools.partial(_attn_kernel, sub=sub, bounded=bounded),
        out_shape=jax.ShapeDtypeStruct((bsz, HEADS, HEAD_V, seq), BF16),
        grid=(bsz, HEADS, seq // tq),
        in_specs=[pl.BlockSpec((1, tq, HEAD_QK), lambda b, h, i: (b, i, h)),
                  pl.BlockSpec((1, seq, HEAD_QK), lambda b, h, i: (b, 0, h)),
                  pl.BlockSpec((1, 1, HEAD_V, seq), lambda b, h, i: (b, h, 0, 0)),
                  pl.BlockSpec((1, cseq, HEAD_QK), lambda b, h, i: (b, 0, h)),
                  pl.BlockSpec((1, 1, HEAD_V, cseq), lambda b, h, i: (b, h, 0, 0))],
        out_specs=pl.BlockSpec((1, 1, HEAD_V, tq), lambda b, h, i: (b, h, 0, i)),
        compiler_params=_params(3),
        name="mla_attention" if bounded else "mla_attention_rowmax",
    )(q, k, vt, kc, vct)


def _score_bound(nope_g, rope_g):
    g2 = lambda g: jnp.max(g * g)
    q2 = NOPE * g2(nope_g[0]) + ROPE * g2(rope_g[0])
    k2 = NOPE * g2(nope_g[1]) + ROPE * g2(rope_g[1])
    return 1.01 * SCORE_SCALE * LOG2E * jnp.sqrt(q2 * k2)


def _mla_layer(x, cx, mod, mod_c, ng, w_in, q_norm, kv_norm, w_uq, w_ukv, nope_g, rope_g, w_out):
    bsz, seq, _ = x.shape
    cseq = cx.shape[1]
    kvc = KV_RANK + ROPE
    w_kv_in = jnp.pad(w_in[:, :kvc], ((0, 0), (0, 128 - ROPE)))
    gate0 = kvc + Q_RANK
    w_in_p = jnp.concatenate([w_kv_in, w_in[:, kvc:gate0], 0.5 * w_in[:, gate0:]], axis=1).astype(BF16)
    w_kv_in = w_kv_in.astype(BF16)
    w_uq_p = jnp.pad(w_uq.reshape(Q_RANK, HEADS, NOPE + ROPE),
                     ((0, 0), (0, 0), (0, HEAD_QK - NOPE - ROPE))).reshape(Q_RANK, HEADS * HEAD_QK).astype(BF16)
    w_ukv_b = w_ukv.astype(BF16)
    pad_g = lambda g: jnp.pad(g, (0, 128 - ROPE)).reshape(1, 128)
    qng, kng = nope_g[0].reshape(1, NOPE), nope_g[1].reshape(1, NOPE)
    qrg, krg = pad_g(rope_g[0]), pad_g(rope_g[1])
    cos, sup, sdn = _rope_tables(seq)
    bound = _score_bound(nope_g, rope_g)
    qshift = jnp.zeros((1, 128), F32).at[0, ROPE].set(-jnp.minimum(bound, ATTN_MAX_SHIFT))

    tr = MLA_ROW_TILE
    win_cols = w_in_p.shape[1]
    row = lambda w: pl.BlockSpec((1, tr, w), lambda b, j: (b, j, 0))
    modspec = pl.BlockSpec((1, 3, D), lambda b, j: (b, 0, 0))
    tab = pl.BlockSpec((tr, 128), lambda b, j: (j, 0))
    q, k, vt, sg = pl.pallas_call(
        _mla_proj_kernel,
        out_shape=(jax.ShapeDtypeStruct((bsz, seq, HEADS * HEAD_QK), BF16),
                   jax.ShapeDtypeStruct((bsz, seq, HEADS * HEAD_QK), BF16),
                   jax.ShapeDtypeStruct((bsz, HEADS, HEAD_V, seq), BF16),
                   jax.ShapeDtypeStruct((bsz, seq, D), BF16)),
        grid=(bsz, seq // tr),
        in_specs=[row(D), modspec, _const_spec((1, D)),
                  _const_spec((D, win_cols)),
                  _const_spec((1, Q_RANK)), _const_spec((1, KV_RANK)),
                  _const_spec((Q_RANK, HEADS * HEAD_QK)), _const_spec((KV_RANK, HEADS * HEAD_QK)),
                  _const_spec((1, NOPE)), _const_spec((1, 128)), _const_spec((1, NOPE)), _const_spec((1, 128)),
                  _const_spec((1, 128)), tab, tab, tab],
        out_specs=(row(HEADS * HEAD_QK), row(HEADS * HEAD_QK),
                   pl.BlockSpec((1, HEADS, HEAD_V, tr), lambda b, j: (b, 0, 0, j)), row(D)),
        compiler_params=_params(2),
        name="mla_proj",
    )(x, mod, ng, w_in_p, q_norm.reshape(1, -1), kv_norm.reshape(1, -1), w_uq_p, w_ukv_b,
      qng, qrg, kng, krg, qshift, cos, sup, sdn)

    trc = min(tr, cseq)
    rowc = lambda w: pl.BlockSpec((1, trc, w), lambda b, j: (b, j, 0))
    kc, vct = pl.pallas_call(
        _mla_ctx_kernel,
        out_shape=(jax.ShapeDtypeStruct((bsz, cseq, HEADS * HEAD_QK), BF16),
                   jax.ShapeDtypeStruct((bsz, HEADS, HEAD_V, cseq), BF16)),
        grid=(bsz, cseq // trc),
        in_specs=[rowc(D), modspec, _const_spec((1, D)),
                  _const_spec((D, KV_RANK + 128)),
                  _const_spec((1, KV_RANK)),
                  _const_spec((KV_RANK, HEADS * HEAD_QK)),
                  _const_spec((1, NOPE)), _const_spec((1, 128))],
        out_specs=(rowc(HEADS * HEAD_QK), pl.BlockSpec((1, HEADS, HEAD_V, trc), lambda b, j: (b, 0, 0, j))),
        compiler_params=_params(2),
        name="mla_ctx_proj",
    )(cx, mod_c, ng, w_kv_in, kv_norm.reshape(1, -1), w_ukv_b, kng, krg)

    ot = lax.cond(bound <= ATTN_MAX_SHIFT,
                  lambda: _attention(q, k, vt, kc, vct, True),
                  lambda: _attention(q, k, vt, kc, vct, False))
    return ot, sg, mod, w_out.astype(BF16)


def kernel(x, c, ctx, c_ctx, norm_g, w_mod, b_mod, cv_w_in, cv_dw, cv_db, cv_ln_g, cv_ln_b, cv_w_out, pl_w_in, pl_w_grp, pl_scale, pl_w_out, ml_w_in, ml_q_norm, ml_kv_norm, ml_w_uq, ml_w_ukv, ml_nope_norm, ml_rope_norm, ml_w_out, ch_w_in, ch_ln_g, ch_ln_b, ch_w_s, ch_b_s, ch_w_out):
    bsz = x.shape[0]
    rows = -(-(bsz + 1) // 8) * 8
    cvec = jnp.concatenate([c, c_ctx[None, :], jnp.zeros((rows - bsz - 1, D), F32)], axis=0)
    mods = _modulation(cvec, w_mod, b_mod)
    row2 = lambda a: a.reshape(1, -1)
    cx = ctx
    pending = None
    for i in range(DEPTH):
        kind, j = i % 4, i // 4
        ctx_out = any(k % 4 == 2 for k in range(i + 1, DEPTH))
        ctx_in = ctx_out or kind == 2
        mod = mods[i, :bsz].reshape(bsz, 3, D)
        mod_c = jnp.broadcast_to(mods[i, bsz].reshape(1, 3, D), (bsz, 3, D))
        ng = row2(norm_g[i])
        if kind == 0:
            args = ((0.5 * cv_w_in[j]).astype(BF16), cv_dw[j], row2(cv_db[j]), row2(0.5 * cv_ln_g[j]),
                    row2(0.5 * cv_ln_b[j]), cv_w_out[j].astype(BF16))
            layer = _conv_layer
        elif kind == 1:
            w_in = jnp.concatenate([pl_w_in[j][:, :D], 0.5 * pl_w_in[j][:, D:]], axis=1)
            args = (w_in.astype(BF16), pl_w_grp[j].astype(BF16), row2(pl_scale[j]), pl_w_out[j].astype(BF16))
            layer = _pool_layer
        elif kind == 3:
            bias = jnp.repeat(ch_b_s[j], D // CHUNK_GROUPS, axis=1)
            w_in = jnp.concatenate([ch_w_in[j][:, :2 * D], 0.5 * ch_w_in[j][:, 2 * D:]], axis=1)
            args = (w_in.astype(BF16), row2(ch_ln_g[j]), row2(ch_ln_b[j]), ch_w_s[j].astype(BF16), bias,
                    ch_w_out[j].astype(BF16))
            layer = _chunk_layer
        if kind == 2:
            assert not ctx_out and i + 1 < DEPTH and (i + 1) % 4 == 3
            pending = _mla_layer(x, cx, mod, mod_c, ng, ml_w_in[j], ml_q_norm[j], ml_kv_norm[j], ml_w_uq[j],
                                 ml_w_ukv[j], ml_nope_norm[j], ml_rope_norm[j], ml_w_out[j])
        else:
            if ctx_out:
                cx = layer(cx, mod_c, ng, *args)
            if kind == 3:
                x = layer(x, mod, ng, *args, pre=pending)
                pending = None
            else:
                x = layer(x, mod, ng, *args)
        del ctx_in
    return x
```

```python
import functools

import numpy as np
import jax
import jax.numpy as jnp
from jax import lax
from jax.experimental import pallas as pl
from jax.experimental.pallas import tpu as pltpu

D = 1024
DEPTH = 4
EPS = 1e-6
GRID_W = 64
CONV_WIDTH = 31
CONV_HALO = 16
CONV_ROW_STRIDE = 4
CONV_LANE_BLOCK = 256
CONV_GROUP_ROWS = 128
POOL_WINDOWS = (2, 4, 8, 16)
POOL_GROUP = D // len(POOL_WINDOWS)
POOL_HALO = 8
POOL_LOG_FROM = 2
HEADS = 8
NOPE = 128
ROPE = 64
HEAD_V = 128
HEAD_QK = 256
Q_RANK = 384
KV_RANK = 256
ROPE_THETA = 10000.0
SCORE_SCALE = (NOPE + ROPE) ** -0.5
LOG2E = 1.4426950408889634
CHUNK = 128
CHUNK_GROUPS = 8
SEQ_TILE = 512
MLA_ROW_TILE = 256
ATTN_Q_TILE = 2048
ATTN_Q_SUB = 512
ATTN_MAX_SHIFT = 50.0
VMEM_LIMIT = 56 * 1024 * 1024

F32 = jnp.float32
BF16 = jnp.bfloat16


def _silu_half(hx):
    return hx + hx * jnp.tanh(hx)


def _glu_half(ha, hb):
    return ha + ha * jnp.tanh(hb)


def _silu(x):
    return _silu_half(0.5 * x)


def _dot(a, b):
    return jnp.dot(a, b, preferred_element_type=F32)


def _dot_nt(a, b):
    return lax.dot_general(a, b, (((1,), (1,)), ((), ())), preferred_element_type=F32)


def _rms(x, g, n=None):
    n = x.shape[-1] if n is None else n
    r = lax.rsqrt(jnp.sum(x * x, axis=-1, keepdims=True) + n * EPS)
    return (x * r) * (g * (n ** 0.5))


def _layernorm(x, g, b):
    n = x.shape[-1]
    xc = x - jnp.sum(x, axis=-1, keepdims=True) * (1.0 / n)
    r = lax.rsqrt(jnp.sum(xc * xc, axis=-1, keepdims=True) + n * EPS)
    return (xc * r) * (g * (n ** 0.5)) + b


def _modulated(x, ng, mod):
    return _rms(x, ng * (1.0 + mod[1:2])) + mod[0:1]


def _const_spec(shape):
    nd = len(shape)
    return pl.BlockSpec(shape, lambda *_: (0,) * nd)


def _params(n_axes):
    return pltpu.CompilerParams(dimension_semantics=("arbitrary",) * n_axes,
                                vmem_limit_bytes=VMEM_LIMIT)


def _mod_kernel(cv_ref, w_ref, b_ref, o_ref):
    s = _silu(cv_ref[...])
    o_ref[0] = _dot(s.astype(BF16), w_ref[0].astype(BF16)) + b_ref[0]


def _modulation(cvec, w_mod, b_mod):
    rows = cvec.shape[0]
    nt = 3
    return pl.pallas_call(
        _mod_kernel,
        out_shape=jax.ShapeDtypeStruct((DEPTH, rows, 3 * D), F32),
        grid=(DEPTH, nt),
        in_specs=[_const_spec((rows, D)),
                  pl.BlockSpec((1, D, D), lambda i, n: (i, 0, n)),
                  pl.BlockSpec((1, 1, D), lambda i, n: (i, 0, n))],
        out_specs=pl.BlockSpec((1, rows, D), lambda i, n: (i, 0, n)),
        compiler_params=_params(2),
        name="modulation",
    )(cvec, w_mod, b_mod.reshape(DEPTH, 1, 3 * D))


def _halo_specs(tl, seq, halo):
    per = tl // halo
    last = seq // halo - 1
    prev = pl.BlockSpec((1, halo, D), lambda b, j: (b, jnp.maximum(j * per - 1, 0), 0))
    main = pl.BlockSpec((1, tl, D), lambda b, j: (b, j, 0))
    nxt = pl.BlockSpec((1, halo, D), lambda b, j: (b, jnp.minimum((j + 1) * per, last), 0))
    return prev, main, nxt


def _row_valid(tl, seq, halo):
    e = lax.broadcasted_iota(jnp.int32, (tl + 2 * halo, 1), 0)
    t = pl.program_id(1) * tl - halo + e
    return (t >= 0) & (t < seq)


def _conv_kernel(xp_ref, xm_ref, xn_ref, mod_ref, ng_ref, win_ref, dw_ref, db_ref,
                 lng_ref, lnb_ref, wout_ref, o_ref, ybuf, zbuf, *, tl, seq):
    halo = CONV_HALO
    mod = mod_ref[0]
    xm = xm_ref[0]
    x_ext = jnp.concatenate([xp_ref[0], xm, xn_ref[0]], axis=0)
    h = _modulated(x_ext, ng_ref[...], mod).astype(BF16)
    valid = _row_valid(tl, seq, halo)

    base = halo - CONV_WIDTH // 2
    rs = CONV_ROW_STRIDE
    lb = CONV_LANE_BLOCK
    for c0 in range(0, D, lb):
        a = _dot(h, win_ref[:, c0:c0 + lb])
        b = _dot(h, win_ref[:, D + c0:D + c0 + lb])
        y = jnp.where(valid, _glu_half(a, b), 0.0)
        for cc in range(0, lb, 128):
            ybuf[(c0 + cc) // 128] = y[:, cc:cc + 128]
        for cc in range(0, lb, 128):
            slab = (c0 + cc) // 128
            taps = [jnp.broadcast_to(dw_ref[k:k + 1, c0 + cc:c0 + cc + 128], (8, 128)) for k in range(CONV_WIDTH)]
            for g0 in range(0, tl, CONV_GROUP_ROWS):
                rows0 = [t0 + q for t0 in range(g0, g0 + CONV_GROUP_ROWS, 8 * rs) for q in range(rs)]
                accs = [None] * len(rows0)
                for k in range(CONV_WIDTH):
                    for n, row in enumerate(rows0):
                        term = taps[k] * ybuf[slab, pl.ds(row + base + k, 8, stride=rs), :]
                        accs[n] = term if accs[n] is None else accs[n] + term
                for n, row in enumerate(rows0):
                    zbuf[slab, pl.ds(row, 8, stride=rs), :] = accs[n]
    sgate = _silu_half(_dot(h[halo:halo + tl], win_ref[:, 2 * D:3 * D]))
    z = jnp.concatenate([zbuf[sl] for sl in range(D // 128)], axis=1) + db_ref[...]
    u = _silu_half(_layernorm(z, lng_ref[...], lnb_ref[...])) * sgate
    o_ref[0] = xm + mod[2:3] * _dot(u.astype(BF16), wout_ref[...])


def _conv_layer(x, mod, ng, w_in, dw, db, ln_g, ln_b, w_out):
    bsz, seq, _ = x.shape
    tl = min(SEQ_TILE, seq)
    prev, main, nxt = _halo_specs(tl, seq, CONV_HALO)
    return pl.pallas_call(
        functools.partial(_conv_kernel, tl=tl, seq=seq),
        out_shape=jax.ShapeDtypeStruct(x.shape, F32),
        grid=(bsz, seq // tl),
        in_specs=[prev, main, nxt,
                  pl.BlockSpec((1, 3, D), lambda b, j: (b, 0, 0)),
                  _const_spec((1, D)),
                  _const_spec((D, 3 * D)),
                  _const_spec((CONV_WIDTH, D)),
                  _const_spec((1, D)), _const_spec((1, D)), _const_spec((1, D)),
                  _const_spec((D, D))],
        out_specs=main,
        scratch_shapes=[pltpu.VMEM((D // 128, tl + 2 * CONV_HALO, 128), F32),
                        pltpu.VMEM((D // 128, tl, 128), F32)],
        compiler_params=_params(2),
        name="conv_layer",
    )(x, x, x, mod, ng, w_in, dw, db, ln_g, ln_b, w_out)


def _pool_kernel(xp_ref, xm_ref, xn_ref, mod_ref, ng_ref, win_ref, wgrp_ref, scale_ref,
                 wout_ref, o_ref, vbuf, pbuf, *, tl, seq):
    halo = POOL_HALO
    mod = mod_ref[0]
    xm = xm_ref[0]
    x_ext = jnp.concatenate([xp_ref[0], xm, xn_ref[0]], axis=0)
    h = _modulated(x_ext, ng_ref[...], mod).astype(BF16)
    v = _dot(h, win_ref[:, 0:D])
    vbuf[...] = jnp.where(_row_valid(tl, seq, halo), v, 0.0)
    sgate = _silu_half(_dot(h[halo:halo + tl], win_ref[:, D:2 * D]))

    n = tl + 2 * halo
    wide0 = POOL_GROUP * POOL_LOG_FROM
    pbuf[0, 0:n - 2] = vbuf[0:n - 2, wide0:D] + vbuf[1:n - 1, wide0:D]
    pbuf[1, 0:n - 4] = pbuf[0, 0:n - 4] + pbuf[0, 2:n - 2]
    pbuf[0, 0:n - 8] = pbuf[1, 0:n - 8] + pbuf[1, 4:n - 4]

    t = pl.program_id(1) * tl + lax.broadcasted_iota(jnp.int32, (tl, 1), 0)
    ys = []
    for k, w in enumerate(POOL_WINDOWS):
        c0 = k * POOL_GROUP
        lo = halo - w // 2
        if w == 8:
            tot = pbuf[0, lo:lo + tl, c0 - wide0:c0 - wide0 + POOL_GROUP]
        elif w == 16:
            tot = (pbuf[0, lo:lo + tl, c0 - wide0:c0 - wide0 + POOL_GROUP]
                   + pbuf[0, lo + 8:lo + 8 + tl, c0 - wide0:c0 - wide0 + POOL_GROUP])
        else:
            tot = jnp.zeros((tl, POOL_GROUP), F32)
            for off in range(w):
                tot = tot + vbuf[lo + off:lo + off + tl, c0:c0 + POOL_GROUP]
        cnt = (jnp.minimum(t + (w - w // 2), seq) - jnp.maximum(t - w // 2, 0)).astype(F32)
        pooled = tot / cnt - vbuf[halo:halo + tl, c0:c0 + POOL_GROUP]
        ys.append(_dot(pooled.astype(BF16), wgrp_ref[k]))
    y = jnp.concatenate(ys, axis=1) * scale_ref[...]
    o_ref[0] = xm + mod[2:3] * _dot((y * sgate).astype(BF16), wout_ref[...])


def _pool_layer(x, mod, ng, w_in, w_grp, scale, w_out):
    bsz, seq, _ = x.shape
    tl = min(SEQ_TILE, seq)
    prev, main, nxt = _halo_specs(tl, seq, POOL_HALO)
    return pl.pallas_call(
        functools.partial(_pool_kernel, tl=tl, seq=seq),
        out_shape=jax.ShapeDtypeStruct(x.shape, F32),
        grid=(bsz, seq // tl),
        in_specs=[prev, main, nxt,
                  pl.BlockSpec((1, 3, D), lambda b, j: (b, 0, 0)),
                  _const_spec((1, D)),
                  _const_spec((D, 2 * D)),
                  _const_spec((len(POOL_WINDOWS), POOL_GROUP, POOL_GROUP)),
                  _const_spec((1, D)),
                  _const_spec((D, D))],
        out_specs=main,
        scratch_shapes=[pltpu.VMEM((tl + 2 * POOL_HALO, D), F32),
                        pltpu.VMEM((2, tl + 2 * POOL_HALO, D - POOL_GROUP * POOL_LOG_FROM), F32)],
        compiler_params=_params(2),
        name="pool_layer",
    )(x, x, x, mod, ng, w_in, w_grp, scale, w_out)


def _chunk_kernel(*refs, tl, has_pre):
    if has_pre:
        x_ref, po_ref, psg_ref, pmod_ref, pw_ref = refs[:5]
        refs = refs[5:]
        o = jnp.concatenate([po_ref[0, hd].astype(F32).T for hd in range(HEADS)], axis=1)
        y0 = o * psg_ref[0].astype(F32)
        x = x_ref[0] + pmod_ref[0][2:3] * _dot(y0.astype(BF16), pw_ref[...])
    else:
        x = refs[0][0]
        refs = refs[1:]
    mod_ref, ng_ref, win_ref, lng_ref, lnb_ref, ws_ref, bs_ref, wout_ref, o_ref = refs
    mod = mod_ref[0]
    h = _modulated(x, ng_ref[...], mod).astype(BF16)
    u = _dot(h, win_ref[:, 0:D])
    v = _layernorm(_dot(h, win_ref[:, D:2 * D]), lng_ref[...], lnb_ref[...]).astype(BF16)
    sgate = _silu_half(_dot(h, win_ref[:, 2 * D:3 * D]))
    nch = tl // CHUNK
    gc = D // CHUNK_GROUPS
    cols = []
    for g in range(CHUNK_GROUPS):
        vg = jnp.concatenate([v[n * CHUNK:(n + 1) * CHUNK, g * gc:(g + 1) * gc] for n in range(nch)], axis=1)
        sg = _dot(ws_ref[g], vg)
        cols.append(jnp.concatenate([sg[:, n * gc:(n + 1) * gc] for n in range(nch)], axis=0))
    s = jnp.concatenate(cols, axis=1) + jnp.concatenate([bs_ref[...]] * nch, axis=0)
    y = u * s * sgate
    o_ref[0] = x + mod[2:3] * _dot(y.astype(BF16), wout_ref[...])


def _chunk_layer(x, mod, ng, w_in, ln_g, ln_b, w_s, b_s, w_out, pre=None):
    bsz, seq, _ = x.shape
    tl = min(SEQ_TILE, seq)
    main = pl.BlockSpec((1, tl, D), lambda b, j: (b, j, 0))
    modspec = pl.BlockSpec((1, 3, D), lambda b, j: (b, 0, 0))
    pre_specs, pre_args = [], []
    if pre is not None:
        pre_specs = [pl.BlockSpec((1, HEADS, HEAD_V, tl), lambda b, j: (b, 0, 0, j)), main, modspec,
                     _const_spec((D, D))]
        pre_args = list(pre)
    return pl.pallas_call(
        functools.partial(_chunk_kernel, tl=tl, has_pre=pre is not None),
        out_shape=jax.ShapeDtypeStruct(x.shape, F32),
        grid=(bsz, seq // tl),
        in_specs=[main] + pre_specs +
                 [modspec,
                  _const_spec((1, D)),
                  _const_spec((D, 3 * D)),
                  _const_spec((1, D)), _const_spec((1, D)),
                  _const_spec((CHUNK_GROUPS, CHUNK, CHUNK)),
                  _const_spec((CHUNK, D)),
                  _const_spec((D, D))],
        out_specs=main,
        compiler_params=_params(2),
        name="chunk_layer",
    )(x, *pre_args, mod, ng, w_in, ln_g, ln_b, w_s, b_s, w_out)


def _rope_tables(seq):
    rows = seq // GRID_W
    row_id = np.repeat(np.arange(rows), GRID_W).astype(np.float32)
    col_id = np.tile(np.arange(GRID_W), rows).astype(np.float32)
    axis_dim = ROPE // 2
    freqs = (np.float32(ROPE_THETA) ** (-np.arange(0, axis_dim, 2, dtype=np.float32) / axis_dim)).astype(np.float32)
    ar = row_id[:, None] * freqs
    ac = col_id[:, None] * freqs
    q = axis_dim // 2
    zeros = np.zeros((seq, q), np.float32)
    pad = np.zeros((seq, 128 - ROPE), np.float32)
    cos = np.concatenate([np.cos(ar), np.cos(ar), np.cos(ac), np.cos(ac), pad], axis=1)
    sin_up = np.concatenate([-np.sin(ar), zeros, -np.sin(ac), zeros, pad], axis=1)
    sin_dn = np.concatenate([zeros, np.sin(ar), zeros, np.sin(ac), pad], axis=1)
    return (jnp.asarray(cos, F32), jnp.asarray(sin_up, F32), jnp.asarray(sin_dn, F32))


def _rope(x, cos, sin_up, sin_dn):
    quarter = ROPE // 4
    up = pltpu.roll(x, 128 - quarter, axis=1)
    dn = pltpu.roll(x, quarter, axis=1)
    return x * cos + up * sin_up + dn * sin_dn


def _mla_keys(ckv, kr, kvn_ref, wukv_ref, kng_ref, krg_ref, tabs, k_ref, v_ref):
    kv = _dot(_rms(ckv, kvn_ref[...]).astype(BF16), wukv_ref[...])
    krn = _rms(kr, krg_ref[...], n=ROPE)
    if tabs is not None:
        krn = _rope(krn, *tabs)
    krn = (krn + (lax.broadcasted_iota(jnp.int32, (1, 128), 1) == ROPE).astype(F32)).astype(BF16)
    kng = kng_ref[...]
    for hd in range(HEADS):
        c0 = hd * HEAD_QK
        k_ref[0, :, c0:c0 + NOPE] = _rms(kv[:, c0:c0 + NOPE], kng).astype(BF16)
        k_ref[0, :, c0 + NOPE:c0 + HEAD_QK] = krn
        v_ref[0, hd] = kv[:, c0 + NOPE:c0 + HEAD_QK].T.astype(BF16)


def _mla_proj_kernel(x_ref, mod_ref, ng_ref, win_ref, qn_ref, kvn_ref, wuq_ref, wukv_ref,
                     qng_ref, qrg_ref, kng_ref, krg_ref, qshift_ref, cos_ref, sup_ref, sdn_ref,
                     q_ref, k_ref, v_ref, sg_ref):
    h = _modulated(x_ref[0], ng_ref[...], mod_ref[0]).astype(BF16)
    p = _dot(h, win_ref[...])
    tabs = (cos_ref[...], sup_ref[...], sdn_ref[...])
    _mla_keys(p[:, 0:KV_RANK], p[:, KV_RANK:KV_RANK + 128], kvn_ref, wukv_ref, kng_ref, krg_ref,
              tabs, k_ref, v_ref)
    c1 = KV_RANK + 128
    q = _dot(_rms(p[:, c1:c1 + Q_RANK], qn_ref[...]).astype(BF16), wuq_ref[...])
    qng = qng_ref[...]
    qrg = qrg_ref[...]
    qs = SCORE_SCALE * LOG2E
    for hd in range(HEADS):
        c0 = hd * HEAD_QK
        q_ref[0, :, c0:c0 + NOPE] = (_rms(q[:, c0:c0 + NOPE], qng) * qs).astype(BF16)
        qr = _rope(_rms(q[:, c0 + NOPE:c0 + HEAD_QK], qrg, n=ROPE), *tabs)
        q_ref[0, :, c0 + NOPE:c0 + HEAD_QK] = (qr * qs + qshift_ref[...]).astype(BF16)
    sg_ref[0] = _silu_half(p[:, c1 + Q_RANK:c1 + Q_RANK + D]).astype(BF16)


def _mla_ctx_kernel(x_ref, mod_ref, ng_ref, win_ref, kvn_ref, wukv_ref, kng_ref, krg_ref,
                    k_ref, v_ref):
    h = _modulated(x_ref[0], ng_ref[...], mod_ref[0]).astype(BF16)
    p = _dot(h, win_ref[...])
    _mla_keys(p[:, 0:KV_RANK], p[:, KV_RANK:KV_RANK + 128], kvn_ref, wukv_ref, kng_ref, krg_ref,
              None, k_ref, v_ref)


def _attn_kernel(q_ref, kl_ref, vlt_ref, kc_ref, vct_ref, o_ref, *, sub, bounded):
    for r0 in range(0, q_ref.shape[1], sub):
        q = q_ref[0, r0:r0 + sub, :]
        s1 = _dot_nt(kl_ref[0], q)
        s2 = _dot_nt(kc_ref[0], q)
        if not bounded:
            m = jnp.maximum(jnp.max(s1, axis=0, keepdims=True), jnp.max(s2, axis=0, keepdims=True))
            s1 = s1 - m
            s2 = s2 - m
        p1 = jnp.exp2(s1)
        p2 = jnp.exp2(s2)
        l = jnp.sum(p1, axis=0, keepdims=True) + jnp.sum(p2, axis=0, keepdims=True)
        o = _dot(vlt_ref[0, 0], p1.astype(BF16)) + _dot(vct_ref[0, 0], p2.astype(BF16))
        o_ref[0, 0, :, r0:r0 + sub] = (o / l).astype(BF16)


def _attention(q, k, vt, kc, vct, bounded):
    bsz, seq, _ = q.shape
    cseq = kc.shape[1]
    tq = min(ATTN_Q_TILE, seq)
    sub = tq if bounded else min(ATTN_Q_SUB, tq)
    return pl.pallas_call(
        functools.partial(_attn_kernel, sub=sub, bounded=bounded),
        out_shape=jax.ShapeDtypeStruct((bsz, HEADS, HEAD_V, seq), BF16),
        grid=(bsz, HEADS, seq // tq),
        in_specs=[pl.BlockSpec((1, tq, HEAD_QK), lambda b, h, i: (b, i, h)),
                  pl.BlockSpec((1, seq, HEAD_QK), lambda b, h, i: (b, 0, h)),
                  pl.BlockSpec((1, 1, HEAD_V, seq), lambda b, h, i: (b, h, 0, 0)),
                  pl.BlockSpec((1, cseq, HEAD_QK), lambda b, h, i: (b, 0, h)),
                  pl.BlockSpec((1, 1, HEAD_V, cseq), lambda b, h, i: (b, h, 0, 0))],
        out_specs=pl.BlockSpec((1, 1, HEAD_V, tq), lambda b, h, i: (b, h, 0, i)),
        compiler_params=_params(3),
        name="mla_attention" if bounded else "mla_attention_rowmax",
    )(q, k, vt, kc, vct)


def _score_bound(nope_g, rope_g):
    g2 = lambda g: jnp.max(g * g)
    q2 = NOPE * g2(nope_g[0]) + ROPE * g2(rope_g[0])
    k2 = NOPE * g2(nope_g[1]) + ROPE * g2(rope_g[1])
    return 1.01 * SCORE_SCALE * LOG2E * jnp.sqrt(q2 * k2)


def _mla_layer(x, cx, mod, mod_c, ng, w_in, q_norm, kv_norm, w_uq, w_ukv, nope_g, rope_g, w_out):
    bsz, seq, _ = x.shape
    cseq = cx.shape[1]
    kvc = KV_RANK + ROPE
    w_kv_in = jnp.pad(w_in[:, :kvc], ((0, 0), (0, 128 - ROPE)))
    gate0 = kvc + Q_RANK
    w_in_p = jnp.concatenate([w_kv_in, w_in[:, kvc:gate0], 0.5 * w_in[:, gate0:]], axis=1).astype(BF16)
    w_kv_in = w_kv_in.astype(BF16)
    w_uq_p = jnp.pad(w_uq.reshape(Q_RANK, HEADS, NOPE + ROPE),
                     ((0, 0), (0, 0), (0, HEAD_QK - NOPE - ROPE))).reshape(Q_RANK, HEADS * HEAD_QK).astype(BF16)
    w_ukv_b = w_ukv.astype(BF16)
    pad_g = lambda g: jnp.pad(g, (0, 128 - ROPE)).reshape(1, 128)
    qng, kng = nope_g[0].reshape(1, NOPE), nope_g[1].reshape(1, NOPE)
    qrg, krg = pad_g(rope_g[0]), pad_g(rope_g[1])
    cos, sup, sdn = _rope_tables(seq)
    bound = _score_bound(nope_g, rope_g)
    qshift = jnp.zeros((1, 128), F32).at[0, ROPE].set(-jnp.minimum(bound, ATTN_MAX_SHIFT))

    tr = MLA_ROW_TILE
    win_cols = w_in_p.shape[1]
    row = lambda w: pl.BlockSpec((1, tr, w), lambda b, j: (b, j, 0))
    modspec = pl.BlockSpec((1, 3, D), lambda b, j: (b, 0, 0))
    tab = pl.BlockSpec((tr, 128), lambda b, j: (j, 0))
    q, k, vt, sg = pl.pallas_call(
        _mla_proj_kernel,
        out_shape=(jax.ShapeDtypeStruct((bsz, seq, HEADS * HEAD_QK), BF16),
                   jax.ShapeDtypeStruct((bsz, seq, HEADS * HEAD_QK), BF16),
                   jax.ShapeDtypeStruct((bsz, HEADS, HEAD_V, seq), BF16),
                   jax.ShapeDtypeStruct((bsz, seq, D), BF16)),
        grid=(bsz, seq // tr),
        in_specs=[row(D), modspec, _const_spec((1, D)),
                  _const_spec((D, win_cols)),
                  _const_spec((1, Q_RANK)), _const_spec((1, KV_RANK)),
                  _const_spec((Q_RANK, HEADS * HEAD_QK)), _const_spec((KV_RANK, HEADS * HEAD_QK)),
                  _const_spec((1, NOPE)), _const_spec((1, 128)), _const_spec((1, NOPE)), _const_spec((1, 128)),
                  _const_spec((1, 128)), tab, tab, tab],
        out_specs=(row(HEADS * HEAD_QK), row(HEADS * HEAD_QK),
                   pl.BlockSpec((1, HEADS, HEAD_V, tr), lambda b, j: (b, 0, 0, j)), row(D)),
        compiler_params=_params(2),
        name="mla_proj",
    )(x, mod, ng, w_in_p, q_norm.reshape(1, -1), kv_norm.reshape(1, -1), w_uq_p, w_ukv_b,
      qng, qrg, kng, krg, qshift, cos, sup, sdn)

    trc = min(tr, cseq)
    rowc = lambda w: pl.BlockSpec((1, trc, w), lambda b, j: (b, j, 0))
    kc, vct = pl.pallas_call(
        _mla_ctx_kernel,
        out_shape=(jax.ShapeDtypeStruct((bsz, cseq, HEADS * HEAD_QK), BF16),
                   jax.ShapeDtypeStruct((bsz, HEADS, HEAD_V, cseq), BF16)),
        grid=(bsz, cseq // trc),
        in_specs=[rowc(D), modspec, _const_spec((1, D)),
                  _const_spec((D, KV_RANK + 128)),
                  _const_spec((1, KV_RANK)),
                  _const_spec((KV_RANK, HEADS * HEAD_QK)),
                  _const_spec((1, NOPE)), _const_spec((1, 128))],
        out_specs=(rowc(HEADS * HEAD_QK), pl.BlockSpec((1, HEADS, HEAD_V, trc), lambda b, j: (b, 0, 0, j))),
        compiler_params=_params(2),
        name="mla_ctx_proj",
    )(cx, mod_c, ng, w_kv_in, kv_norm.reshape(1, -1), w_ukv_b, kng, krg)

    ot = lax.cond(bound <= ATTN_MAX_SHIFT,
                  lambda: _attention(q, k, vt, kc, vct, True),
                  lambda: _attention(q, k, vt, kc, vct, False))
    return ot, sg, mod, w_out.astype(BF16)


def kernel(x, c, ctx, c_ctx, norm_g, w_mod, b_mod, cv_w_in, cv_dw, cv_db, cv_ln_g, cv_ln_b, cv_w_out, pl_w_in, pl_w_grp, pl_scale, pl_w_out, ml_w_in, ml_q_norm, ml_kv_norm, ml_w_uq, ml_w_ukv, ml_nope_norm, ml_rope_norm, ml_w_out, ch_w_in, ch_ln_g, ch_ln_b, ch_w_s, ch_b_s, ch_w_out):
    bsz = x.shape[0]
    rows = -(-(bsz + 1) // 8) * 8
    cvec = jnp.concatenate([c, c_ctx[None, :], jnp.zeros((rows - bsz - 1, D), F32)], axis=0)
    mods = _modulation(cvec, w_mod, b_mod)
    row2 = lambda a: a.reshape(1, -1)
    cx = ctx
    pending = None
    for i in range(DEPTH):
        kind, j = i % 4, i // 4
        ctx_out = any(k % 4 == 2 for k in range(i + 1, DEPTH))
        ctx_in = ctx_out or kind == 2
        mod = mods[i, :bsz].reshape(bsz, 3, D)
        mod_c = jnp.broadcast_to(mods[i, bsz].reshape(1, 3, D), (bsz, 3, D))
        ng = row2(norm_g[i])
        if kind == 0:
            args = ((0.5 * cv_w_in[j]).astype(BF16), cv_dw[j], row2(cv_db[j]), row2(0.5 * cv_ln_g[j]),
                    row2(0.5 * cv_ln_b[j]), cv_w_out[j].astype(BF16))
            layer = _conv_layer
        elif kind == 1:
            w_in = jnp.concatenate([pl_w_in[j][:, :D], 0.5 * pl_w_in[j][:, D:]], axis=1)
            args = (w_in.astype(BF16), pl_w_grp[j].astype(BF16), row2(pl_scale[j]), pl_w_out[j].astype(BF16))
            layer = _pool_layer
        elif kind == 3:
            bias = jnp.repeat(ch_b_s[j], D // CHUNK_GROUPS, axis=1)
            w_in = jnp.concatenate([ch_w_in[j][:, :2 * D], 0.5 * ch_w_in[j][:, 2 * D:]], axis=1)
            args = (w_in.astype(BF16), row2(ch_ln_g[j]), row2(ch_ln_b[j]), ch_w_s[j].astype(BF16), bias,
                    ch_w_out[j].astype(BF16))
            layer = _chunk_layer
        if kind == 2:
            assert not ctx_out and i + 1 < DEPTH and (i + 1) % 4 == 3
            pending = _mla_layer(x, cx, mod, mod_c, ng, ml_w_in[j], ml_q_norm[j], ml_kv_norm[j], ml_w_uq[j],
                                 ml_w_ukv[j], ml_nope_norm[j], ml_rope_norm[j], ml_w_out[j])
        else:
            if ctx_out:
                cx = layer(cx, mod_c, ng, *args)
            if kind == 3:
                x = layer(x, mod, ng, *args, pre=pending)
                pending = None
            else:
                x = layer(x, mod, ng, *args)
        del ctx_in
    return x
```
